```python
import jax, jax.numpy as jnp
from jax import lax
import numpy as np

D_MODEL = 1024
BATCH = 32
SEQ = 2048
DEPTH = 2
DEC_BATCH = 8
DEC_SEQ = 4096
PAST_LEN = 128

RET_HEADS = 4
RET_DK = 128
RET_DV = 128
RET_QK_WIDTH = 512
RET_WIDTH = 512
RET_CHUNK = 128
RET_LOG2_FWD = (-5.0, -6.0, -7.0, -8.0)
RET_LOG2_BWD = (-5.5, -6.5, -7.5, -8.5)
MLA_HEADS = 4
MLA_NOPE = 128
MLA_ROPE = 64
MLA_QK = 192
MLA_V = 128
MLA_WIDTH = 512
Q_LORA = 256
KV_LORA = 128
Q_BLOCK = 128
ROPE_BASE = 10000.0
NORM_EPS = 1e-6
IN_SPLITS = (RET_QK_WIDTH, RET_QK_WIDTH, RET_WIDTH, RET_WIDTH, Q_LORA, KV_LORA, MLA_ROPE, MLA_WIDTH, D_MODEL, D_MODEL)
IN_WIDTH = 2 * RET_QK_WIDTH + 2 * RET_WIDTH + Q_LORA + KV_LORA + MLA_ROPE + MLA_WIDTH + 2 * D_MODEL

kernel_name = "hybrid_retention_mla_encoder"


def rms_norm(x, g):
    xf = x.astype(jnp.float32)
    y = xf * lax.rsqrt(jnp.mean(xf * xf, axis=-1, keepdims=True) + NORM_EPS)
    return (y * g.astype(jnp.float32)).astype(x.dtype)


def rope(x):
    S, d = x.shape[1], x.shape[-1]
    inv = ROPE_BASE ** (-jnp.arange(0, d, 2, dtype=jnp.float32) / d)
    ang = jnp.arange(S, dtype=jnp.float32)[:, None] * inv[None, :]
    cos = jnp.cos(ang)[None, :, None, :]
    sin = jnp.sin(ang)[None, :, None, :]
    xf = x.astype(jnp.float32)
    x1, x2 = xf[..., : d // 2], xf[..., d // 2:]
    return jnp.concatenate([x1 * cos - x2 * sin, x1 * sin + x2 * cos], axis=-1).astype(x.dtype)


def retention_dir(q, k, v, log_gamma, include_diag):
    b, L, H, dk = q.shape
    dv = v.shape[-1]
    n = L // RET_CHUNK

    def chunks(t):
        return t.reshape(b, n, RET_CHUNK, H, t.shape[-1]).transpose(1, 0, 3, 2, 4)

    pos = jnp.arange(RET_CHUNK, dtype=jnp.float32)
    diff = pos[:, None] - pos[None, :]
    mask = (diff >= 0) if include_diag else (diff > 0)
    intra_decay = jnp.where(mask, jnp.exp(log_gamma[:, None, None] * jnp.maximum(diff, 0.0)), 0.0)
    xi = jnp.exp(log_gamma[:, None] * (pos + 1.0))[..., None]
    zeta = jnp.exp(log_gamma[:, None] * (RET_CHUNK - 1.0 - pos))[..., None]
    chunk_decay = jnp.exp(log_gamma * RET_CHUNK)[:, None, None]

    def step(state, inp):
        qi, ki, vi = inp
        s = jnp.einsum('bhqd,bhkd->bhqk', qi, ki) * intra_decay
        out = jnp.einsum('bhqk,bhkv->bhqv', s, vi) + jnp.einsum('bhqd,bhdv->bhqv', qi * xi, state)
        state = state * chunk_decay + jnp.einsum('bhkd,bhkv->bhdv', ki * zeta, vi)
        return state, out

    init = jnp.zeros((b, H, dk, dv), jnp.float32)
    _, out = lax.scan(step, init, (chunks(q), chunks(k), chunks(v)))
    return out.transpose(1, 0, 3, 2, 4).reshape(b, L, H, dv)


def mla_attention(q, k, v):
    b, S, H, dq = q.shape
    nb = S // Q_BLOCK
    qb = q.reshape(b, nb, Q_BLOCK, H, dq).transpose(1, 0, 2, 3, 4)
    scale = dq ** -0.5

    def block(qi):
        s = jnp.einsum('bqhd,bkhd->bhqk', qi, k).astype(jnp.float32) * scale
        p = jax.nn.softmax(s, axis=-1)
        return jnp.einsum('bhqk,bkhv->bqhv', p.astype(v.dtype), v)

    o = lax.map(block, qb)
    return o.transpose(1, 0, 2, 3, 4).reshape(b, S, H * MLA_V)


def mixer_layer(x, norm_g, w_in, ret_gn_g, q_norm_g, kv_norm_g, w_uq, w_ukv, w_br_ret, w_br_mla, w_out):
    b, S, _ = x.shape
    h = rms_norm(x, norm_g)
    z = h @ w_in
    idx = [int(i) for i in np.cumsum(IN_SPLITS)[:-1]]
    rq, rk, rv, rg, cq, ckv, kpe, mg, g_ret, g_mla = jnp.split(z, idx, axis=-1)

    rq = rope(rq.reshape(b, S, RET_HEADS, RET_DK)).astype(jnp.float32)
    rk = rope(rk.reshape(b, S, RET_HEADS, RET_DK)).astype(jnp.float32) * (RET_DK ** -0.5)
    rv = rv.reshape(b, S, RET_HEADS, RET_DV).astype(jnp.float32)
    lg_f = jnp.log1p(-jnp.exp2(jnp.array(RET_LOG2_FWD, jnp.float32)))
    lg_b = jnp.log1p(-jnp.exp2(jnp.array(RET_LOG2_BWD, jnp.float32)))
    o_f = retention_dir(rq, rk, rv, lg_f, True)
    o_b = retention_dir(rq[:, ::-1], rk[:, ::-1], rv[:, ::-1], lg_b, False)[:, ::-1]
    o = o_f + o_b
    mu = jnp.mean(o, axis=-1, keepdims=True)
    var = jnp.mean(jnp.square(o - mu), axis=-1, keepdims=True)
    o = ((o - mu) * lax.rsqrt(var + NORM_EPS)).reshape(b, S, RET_WIDTH) * ret_gn_g.astype(jnp.float32)
    ret_out = (o.astype(x.dtype) * jax.nn.silu(rg)) @ w_br_ret

    cq = rms_norm(cq, q_norm_g)
    q = (cq @ w_uq).reshape(b, S, MLA_HEADS, MLA_QK)
    q = jnp.concatenate([q[..., :MLA_NOPE], rope(q[..., MLA_NOPE:])], axis=-1)
    ckv = rms_norm(ckv, kv_norm_g)
    kv = (ckv @ w_ukv).reshape(b, S, MLA_HEADS, MLA_NOPE + MLA_V)
    k_nope, v = kv[..., :MLA_NOPE], kv[..., MLA_NOPE:]
    k_pe = rope(kpe.reshape(b, S, 1, MLA_ROPE))
    k = jnp.concatenate([k_nope, jnp.broadcast_to(k_pe, (b, S, MLA_HEADS, MLA_ROPE))], axis=-1)
    a = mla_attention(q, k, v)
    mla_out = (a * jax.nn.silu(mg)) @ w_br_mla

    merged = jax.nn.sigmoid(g_ret) * ret_out + jax.nn.sigmoid(g_mla) * mla_out
    return x + merged @ w_out


def trunk(x, norm_g, w_in, ret_gn_g, q_norm_g, kv_norm_g, w_uq, w_ukv, w_br_ret, w_br_mla, w_out, final_norm_g):
    for i in range(DEPTH):
        x = mixer_layer(x, norm_g[i], w_in[i], ret_gn_g[i], q_norm_g[i], kv_norm_g[i], w_uq[i], w_ukv[i],
                        w_br_ret[i], w_br_mla[i], w_out[i])
    return rms_norm(x, final_norm_g)


def setup_inputs(seed: int = 0) -> dict:
    key = jax.random.key(seed)
    ks = jax.random.split(key, 16)
    f = jnp.float32

    def nrm(k, shape, fan_in):
        return jax.random.normal(k, shape, f) * (fan_in ** -0.5)

    def gain(k, shape):
        return 1.0 + 0.02 * jax.random.normal(k, shape, f)

    return {
        "x_prompt": jax.random.normal(ks[0], (BATCH, SEQ, D_MODEL), f),
        "x_sample": jax.random.normal(ks[1], (DEC_BATCH, DEC_SEQ, D_MODEL), f),
        "norm_g": gain(ks[2], (DEPTH, D_MODEL)),
        "w_in": nrm(ks[3], (DEPTH, D_MODEL, IN_WIDTH), D_MODEL),
        "ret_gn_g": gain(ks[4], (DEPTH, RET_WIDTH)),
        "q_norm_g": gain(ks[5], (DEPTH, Q_LORA)),
        "kv_norm_g": gain(ks[6], (DEPTH, KV_LORA)),
        "w_uq": nrm(ks[7], (DEPTH, Q_LORA, MLA_HEADS * MLA_QK), Q_LORA),
        "w_ukv": nrm(ks[8], (DEPTH, KV_LORA, MLA_HEADS * (MLA_NOPE + MLA_V)), KV_LORA),
        "w_br_ret": nrm(ks[9], (DEPTH, RET_WIDTH, D_MODEL), RET_WIDTH),
        "w_br_mla": nrm(ks[10], (DEPTH, MLA_WIDTH, D_MODEL), MLA_WIDTH),
        "w_out": nrm(ks[11], (DEPTH, D_MODEL, D_MODEL), D_MODEL),
        "final_norm_g": gain(ks[12], (D_MODEL,)),
    }


def reference(x_prompt, x_sample, norm_g, w_in, ret_gn_g, q_norm_g, kv_norm_g, w_uq, w_ukv, w_br_ret, w_br_mla, w_out, final_norm_g):
    y_prompt = trunk(x_prompt, norm_g, w_in, ret_gn_g, q_norm_g, kv_norm_g, w_uq, w_ukv, w_br_ret, w_br_mla, w_out, final_norm_g)
    y_sample = trunk(x_sample, norm_g, w_in, ret_gn_g, q_norm_g, kv_norm_g, w_uq, w_ukv, w_br_ret, w_br_mla, w_out, final_norm_g)
    return (y_prompt, y_sample)
```

```python
import functools
import math

import jax
import jax.numpy as jnp
import numpy as np
from jax import lax
from jax.experimental import pallas as pl
from jax.experimental.pallas import tpu as pltpu

D_MODEL = 1024
DEPTH = 2
RET_HEADS = 4
RET_DK = 128
RET_DV = 128
RET_WIDTH = RET_HEADS * RET_DV
RET_LOG2_FWD = (-5.0, -6.0, -7.0, -8.0)
RET_LOG2_BWD = (-5.5, -6.5, -7.5, -8.5)
MLA_HEADS = 4
MLA_NOPE = 128
MLA_ROPE = 64
MLA_QK = MLA_NOPE + MLA_ROPE
MLA_V = 128
MLA_WIDTH = MLA_HEADS * MLA_V
Q_LORA = 256
KV_LORA = 128
ROPE_BASE = 10000.0
NORM_EPS = 1e-6

LANES = 128
MLA_QK_PAD = 2 * LANES
_OFF = np.cumsum((0, 512, 512, 512, 512, Q_LORA, KV_LORA, MLA_ROPE, MLA_WIDTH, D_MODEL, D_MODEL))
(O_RQ, O_RK, O_RV, O_RG, O_CQ, O_CKV, O_KPE, O_MG, O_GRET, O_GMLA, _) = [int(v) for v in _OFF]
QKV_COLS = 3 * RET_WIDTH + Q_LORA + KV_LORA + LANES
GATE_COLS = RET_WIDTH + MLA_WIDTH + 2 * D_MODEL

ROW_TILE = 512
RET_CHUNK = 256
ATT_Q_TILE = 512
ATT_K_TILE = 512
VMEM_LIMIT = 56 * 1024 * 1024

_NT = (((1,), (1,)), ((), ()))
_TN = (((0,), (0,)), ((), ()))


def _bf16(x):
    return x.astype(jnp.bfloat16)


def _dot(a, b):
    return jnp.dot(a, b, preferred_element_type=jnp.float32)


def _rms(x, g):
    return x * lax.rsqrt(jnp.mean(x * x, axis=-1, keepdims=True) + NORM_EPS) * g


def _resident(shape):
    return pl.BlockSpec(shape, lambda *_: (0,) * len(shape))


def _in_proj_kernel(x_ref, ng_ref, w_ref, qg_ref, kvg_ref, wuq_ref, wukv_ref,
                    cosr_ref, sinr_ref, cosp_ref, sinlo_ref, sinhi_ref,
                    rq_ref, rk_ref, rv_ref, q_ref, kn_ref, v_ref, kpe_ref):
    h = _bf16(_rms(x_ref[...], ng_ref[...]))
    z = _dot(h, w_ref[...])

    cosr, sinr = cosr_ref[...], sinr_ref[...]
    k_scale = RET_DK ** -0.5
    for hd in range(RET_HEADS):
        sl = slice(hd * RET_DK, (hd + 1) * RET_DK)
        xq = z[:, sl]
        rq_ref[:, sl] = _bf16(xq * cosr + pltpu.roll(xq, RET_DK // 2, 1) * sinr)
        xk = z[:, RET_WIDTH + hd * RET_DK: RET_WIDTH + (hd + 1) * RET_DK]
        rk_ref[:, sl] = _bf16((xk * cosr + pltpu.roll(xk, RET_DK // 2, 1) * sinr) * k_scale)
    rv_ref[...] = _bf16(z[:, 2 * RET_WIDTH: 3 * RET_WIDTH])

    cosp, sinlo, sinhi = cosp_ref[...], sinlo_ref[...], sinhi_ref[...]

    def rope64(pe):
        return (pe * cosp + pltpu.roll(pe, LANES - MLA_ROPE // 2, 1) * sinlo
                + pltpu.roll(pe, MLA_ROPE // 2, 1) * sinhi)

    c0 = 3 * RET_WIDTH
    cq = _bf16(_rms(z[:, c0: c0 + Q_LORA], qg_ref[...]))
    q = _dot(cq, wuq_ref[...])
    q_scale = MLA_QK ** -0.5
    for hd in range(MLA_HEADS):
        b0 = hd * MLA_QK_PAD
        q_ref[:, b0: b0 + LANES] = _bf16(q[:, b0: b0 + LANES] * q_scale)
        q_ref[:, b0 + LANES: b0 + 2 * LANES] = _bf16(rope64(q[:, b0 + LANES: b0 + 2 * LANES]) * q_scale)

    c1 = c0 + Q_LORA
    ckv = _bf16(_rms(z[:, c1: c1 + KV_LORA], kvg_ref[...]))
    kv = _dot(ckv, wukv_ref[...])
    kn_ref[...] = _bf16(kv[:, :MLA_HEADS * MLA_NOPE])
    v_ref[...] = _bf16(kv[:, MLA_HEADS * MLA_NOPE:])
    kpe_ref[...] = _bf16(rope64(z[:, c1 + KV_LORA: c1 + KV_LORA + LANES]))


def _in_proj(x2d, seq, ng, w_qkv, qg, kvg, wuq, wukv, tabs):
    tokens = x2d.shape[0]
    tm = ROW_TILE
    pos_tiles = seq // tm
    row = lambda width: pl.BlockSpec((tm, width), lambda i: (i, 0))
    pos = pl.BlockSpec((tm, LANES), lambda i: (i % pos_tiles, 0))
    bf = lambda width: jax.ShapeDtypeStruct((tokens, width), jnp.bfloat16)
    return pl.pallas_call(
        _in_proj_kernel,
        grid=(tokens // tm,),
        in_specs=[row(D_MODEL), _resident((1, D_MODEL)), _resident(w_qkv.shape),
                  _resident((1, Q_LORA)), _resident((1, KV_LORA)),
                  _resident(wuq.shape), _resident(wukv.shape), pos, pos, pos, pos, pos],
        out_specs=[row(RET_WIDTH), row(RET_WIDTH), row(RET_WIDTH), row(MLA_HEADS * MLA_QK_PAD),
                   row(MLA_HEADS * MLA_NOPE), row(MLA_WIDTH), row(LANES)],
        out_shape=[bf(RET_WIDTH), bf(RET_WIDTH), bf(RET_WIDTH), bf(MLA_HEADS * MLA_QK_PAD),
                   bf(MLA_HEADS * MLA_NOPE), bf(MLA_WIDTH), bf(LANES)],
        compiler_params=pltpu.CompilerParams(dimension_semantics=("parallel",),
                                             vmem_limit_bytes=VMEM_LIMIT),
        name="in_proj",
    )(x2d, ng, w_qkv, qg, kvg, wuq, wukv, *tabs)


def _retention_kernel(q_ref, k_ref, v_ref, dmask_ref, qdec_ref, kdec_ref, sdec_ref, gn_ref,
                      o_ref, local_ref, state_ref):
    c = RET_CHUNK
    n = q_ref.shape[0] // c
    kdec = kdec_ref[...]

    def local_kv(i, carry):
        rows = pl.ds(pl.multiple_of(i * c, c), c)
        kf = k_ref[rows, :].astype(jnp.float32)
        kz = jnp.concatenate([_bf16(kf * kdec[:, :RET_DK]), _bf16(kf * kdec[:, RET_DK:])], axis=1)
        local_ref[i] = lax.dot_general(kz, v_ref[rows, :], _TN,
                                       preferred_element_type=jnp.float32)
        return carry

    lax.fori_loop(0, n, local_kv, 0)

    sdec = sdec_ref[...]

    def scan_fwd(i, st):
        state_ref[i, :RET_DK, :] = _bf16(st)
        return st * sdec[:RET_DK] + local_ref[i, :RET_DK, :]

    def scan_bwd(j, st):
        i = n - 1 - j
        state_ref[i, RET_DK:, :] = _bf16(st)
        return st * sdec[RET_DK:] + local_ref[i, RET_DK:, :]

    zero = jnp.zeros((RET_DK, RET_DV), jnp.float32)
    lax.fori_loop(0, n, scan_fwd, zero)
    lax.fori_loop(0, n, scan_bwd, zero)

    dmask = dmask_ref[...]
    qdec = qdec_ref[...]
    gain = gn_ref[...]

    def outputs(i, carry):
        rows = pl.ds(pl.multiple_of(i * c, c), c)
        qc, kc, vc = q_ref[rows, :], k_ref[rows, :], v_ref[rows, :]
        s = lax.dot_general(qc, kc, _NT, preferred_element_type=jnp.float32) * dmask
        qf = qc.astype(jnp.float32)
        qx = jnp.concatenate([_bf16(qf * qdec[:, :RET_DK]), _bf16(qf * qdec[:, RET_DK:])], axis=1)
        o = _dot(_bf16(s), vc) + _dot(qx, state_ref[i])
        mu = jnp.mean(o, axis=-1, keepdims=True)
        d = o - mu
        var = jnp.mean(d * d, axis=-1, keepdims=True)
        o_ref[rows, :] = _bf16(d * lax.rsqrt(var + NORM_EPS) * gain)
        return carry

    lax.fori_loop(0, n, outputs, 0)


def _retention(rq, rk, rv, gn, dec):
    b, seq, _ = rq.shape
    c = RET_CHUNK
    n = seq // c
    head = pl.BlockSpec((None, seq, RET_DK), lambda i, h: (i, 0, h))
    per_head = lambda r, w: pl.BlockSpec((None, r, w), lambda i, h: (h, 0, 0))
    return pl.pallas_call(
        _retention_kernel,
        grid=(b, RET_HEADS),
        in_specs=[head, head, head, per_head(c, c), per_head(c, 2 * RET_DK), per_head(c, 2 * RET_DK),
                  per_head(2 * RET_DK, RET_DV), pl.BlockSpec((1, RET_DV), lambda i, h: (0, h))],
        out_specs=head,
        out_shape=jax.ShapeDtypeStruct((b, seq, RET_WIDTH), jnp.bfloat16),
        scratch_shapes=[pltpu.VMEM((n, 2 * RET_DK, RET_DV), jnp.float32),
                        pltpu.VMEM((n, 2 * RET_DK, RET_DV), jnp.bfloat16)],
        compiler_params=pltpu.CompilerParams(dimension_semantics=("parallel", "arbitrary"),
                                             vmem_limit_bytes=VMEM_LIMIT),
        name="retention",
    )(rq, rk, rv, *dec, gn)


def _retention_tables():
    c = RET_CHUNK
    lg_f = jnp.log1p(-jnp.exp2(jnp.array(RET_LOG2_FWD, jnp.float32)))[:, None, None]
    lg_b = jnp.log1p(-jnp.exp2(jnp.array(RET_LOG2_BWD, jnp.float32)))[:, None, None]
    pos = jnp.arange(c, dtype=jnp.float32)
    diff = pos[:, None] - pos[None, :]
    dmask = jnp.where(diff >= 0, jnp.exp(lg_f * jnp.maximum(diff, 0.0)),
                      jnp.exp(lg_b * jnp.maximum(-diff, 0.0)))
    col = pos[None, :, None]
    wide = lambda f, b_: jnp.concatenate([jnp.broadcast_to(f, (RET_HEADS, c, RET_DK)),
                                          jnp.broadcast_to(b_, (RET_HEADS, c, RET_DK))], axis=2)
    qdec = wide(jnp.exp(lg_f * (col + 1.0)), jnp.exp(lg_b * (c - col)))
    kdec = wide(jnp.exp(lg_f * (c - 1.0 - col)), jnp.exp(lg_b * col))
    sdec = jnp.concatenate([jnp.broadcast_to(jnp.exp(lg_f * c), (RET_HEADS, RET_DK, RET_DV)),
                            jnp.broadcast_to(jnp.exp(lg_b * c), (RET_HEADS, RET_DK, RET_DV))], axis=1)
    return dmask, qdec, kdec, sdec


def _attention_kernel(q_ref, kn_ref, kpe_ref, v_ref, o_ref, kcat_ref):
    seq = q_ref.shape[0]
    tq, tk = ATT_Q_TILE, ATT_K_TILE
    kcat_ref[:, :LANES] = kn_ref[...]
    kcat_ref[:, LANES:] = kpe_ref[...]

    def q_tile(i, carry):
        qrows = pl.ds(pl.multiple_of(i * tq, tq), tq)
        q = q_ref[qrows, :]

        def k_tile(j, st):
            m, l, acc = st
            krows = pl.ds(pl.multiple_of(j * tk, tk), tk)
            s = lax.dot_general(q, kcat_ref[krows, :], _NT, preferred_element_type=jnp.float32)
            m_new = jnp.maximum(m, jnp.max(s, axis=-1, keepdims=True))
            alpha = jnp.exp(m - m_new)
            p = jnp.exp(s - m_new)
            l = alpha * l + jnp.sum(p, axis=-1, keepdims=True)
            acc = alpha * acc + _dot(_bf16(p), v_ref[krows, :])
            return m_new, l, acc

        init = (jnp.full((tq, 1), -jnp.inf, jnp.float32), jnp.zeros((tq, 1), jnp.float32),
                jnp.zeros((tq, MLA_V), jnp.float32))
        _, l, acc = lax.fori_loop(0, seq // tk, k_tile, init)
        o_ref[qrows, :] = _bf16(acc / l)
        return carry

    lax.fori_loop(0, seq // tq, q_tile, 0)


def _attention(q, kn, kpe, v):
    b, seq, _ = q.shape
    head = lambda w: pl.BlockSpec((None, seq, w), lambda i, h: (i, 0, h))
    return pl.pallas_call(
        _attention_kernel,
        grid=(b, MLA_HEADS),
        in_specs=[head(MLA_QK_PAD), head(MLA_NOPE),
                  pl.BlockSpec((None, seq, LANES), lambda i, h: (i, 0, 0)), head(MLA_V)],
        out_specs=head(MLA_V),
        out_shape=jax.ShapeDtypeStruct((b, seq, MLA_WIDTH), jnp.bfloat16),
        scratch_shapes=[pltpu.VMEM((seq, MLA_QK_PAD), jnp.bfloat16)],
        compiler_params=pltpu.CompilerParams(dimension_semantics=("parallel", "arbitrary"),
                                             vmem_limit_bytes=VMEM_LIMIT),
        name="attention",
    )(q, kn, kpe, v)


def _out_proj_kernel(x_ref, ret_ref, att_ref, ng_ref, wg_ref, wbr_ref, wbm_ref, wo_ref, fg_ref,
                     y_ref, *, final_norm):
    x = x_ref[...]
    h = _bf16(_rms(x, ng_ref[...]))
    g = _dot(h, wg_ref[...])
    rg = g[:, :RET_WIDTH]
    mg = g[:, RET_WIDTH: RET_WIDTH + MLA_WIDTH]
    g_ret = g[:, RET_WIDTH + MLA_WIDTH: RET_WIDTH + MLA_WIDTH + D_MODEL]
    g_mla = g[:, RET_WIDTH + MLA_WIDTH + D_MODEL:]
    ret_out = _dot(_bf16(ret_ref[...].astype(jnp.float32) * (rg * jax.nn.sigmoid(rg))), wbr_ref[...])
    mla_out = _dot(_bf16(att_ref[...].astype(jnp.float32) * (mg * jax.nn.sigmoid(mg))), wbm_ref[...])
    merged = jax.nn.sigmoid(g_ret) * ret_out + jax.nn.sigmoid(g_mla) * mla_out
    y = x + _dot(_bf16(merged), wo_ref[...])
    if final_norm:
        y = _rms(y, fg_ref[...])
    y_ref[...] = y


def _out_proj(x2d, ret, att, ng, w_gate, w_br_ret, w_br_mla, w_out, fg, final_norm):
    tokens = x2d.shape[0]
    tm = ROW_TILE
    row = lambda width: pl.BlockSpec((tm, width), lambda i: (i, 0))
    return pl.pallas_call(
        functools.partial(_out_proj_kernel, final_norm=final_norm),
        grid=(tokens // tm,),
        in_specs=[row(D_MODEL), row(RET_WIDTH), row(MLA_WIDTH), _resident((1, D_MODEL)),
                  _resident(w_gate.shape), _resident(w_br_ret.shape), _resident(w_br_mla.shape),
                  _resident(w_out.shape), _resident((1, D_MODEL))],
        out_specs=row(D_MODEL),
        out_shape=jax.ShapeDtypeStruct((tokens, D_MODEL), jnp.float32),
        compiler_params=pltpu.CompilerParams(dimension_semantics=("parallel",),
                                             vmem_limit_bytes=VMEM_LIMIT),
        name="out_proj",
    )(x2d, ret, att, ng, w_gate, w_br_ret, w_br_mla, w_out, fg)


def _rope_tables(seq):
    posn = jnp.arange(seq, dtype=jnp.float32)[:, None]

    def angles(d):
        inv = ROPE_BASE ** (-jnp.arange(0, d, 2, dtype=jnp.float32) / d)
        return posn * inv[None, :]

    ar = angles(RET_DK)
    cos_r = jnp.concatenate([jnp.cos(ar), jnp.cos(ar)], axis=1)
    sin_r = jnp.concatenate([-jnp.sin(ar), jnp.sin(ar)], axis=1)
    ap = angles(MLA_ROPE)
    z32 = jnp.zeros_like(ap)
    z64 = jnp.zeros((seq, LANES - MLA_ROPE), jnp.float32)
    cos_p = jnp.concatenate([jnp.cos(ap), jnp.cos(ap), z64], axis=1)
    sin_lo = jnp.concatenate([-jnp.sin(ap), z32, z64], axis=1)
    sin_hi = jnp.concatenate([z32, jnp.sin(ap), z64], axis=1)
    return cos_r, sin_r, cos_p, sin_lo, sin_hi


def _prep_layer(w_in, w_uq, w_ukv):
    zpad = jnp.zeros((D_MODEL, LANES - MLA_ROPE), w_in.dtype)
    w_qkv = jnp.concatenate([w_in[:, O_RQ:O_RG], w_in[:, O_CQ:O_MG], zpad], axis=1)
    w_gate = jnp.concatenate([w_in[:, O_RG:O_CQ], w_in[:, O_MG:]], axis=1)
    uq = w_uq.reshape(Q_LORA, MLA_HEADS, MLA_QK)
    uq = jnp.concatenate([uq, jnp.zeros((Q_LORA, MLA_HEADS, MLA_QK_PAD - MLA_QK), w_uq.dtype)], axis=2)
    ukv = w_ukv.reshape(KV_LORA, MLA_HEADS, MLA_NOPE + MLA_V)
    ukv = jnp.concatenate([ukv[:, :, :MLA_NOPE].reshape(KV_LORA, -1),
                           ukv[:, :, MLA_NOPE:].reshape(KV_LORA, -1)], axis=1)
    return _bf16(w_qkv), _bf16(w_gate), _bf16(uq.reshape(Q_LORA, -1)), _bf16(ukv)


def _trunk(x, layers, final_norm_g, ret_tabs):
    b, seq, _ = x.shape
    rope_tabs = _rope_tables(seq)
    x2d = x.reshape(b * seq, D_MODEL)
    for i, lw in enumerate(layers):
        rq, rk, rv, q, kn, v, kpe = _in_proj(x2d, seq, lw["ng"], lw["w_qkv"], lw["qg"], lw["kvg"],
                                             lw["wuq"], lw["wukv"], rope_tabs)
        sh = lambda t: t.reshape(b, seq, t.shape[-1])
        ret = _retention(sh(rq), sh(rk), sh(rv), lw["gn"], ret_tabs)
        att = _attention(sh(q), sh(kn), sh(kpe), sh(v))
        x2d = _out_proj(x2d, ret.reshape(b * seq, -1), att.reshape(b * seq, -1), lw["ng"], lw["w_gate"],
                        lw["wbr"], lw["wbm"], lw["wo"], final_norm_g, final_norm=(i == DEPTH - 1))
    return x2d.reshape(b, seq, D_MODEL)


def kernel(x_prompt, x_sample, norm_g, w_in, ret_gn_g, q_norm_g, kv_norm_g, w_uq, w_ukv, w_br_ret, w_br_mla,
           w_out, final_norm_g):
    layers = []
    for i in range(DEPTH):
        w_qkv, w_gate, wuq, wukv = _prep_layer(w_in[i], w_uq[i], w_ukv[i])
        layers.append(dict(ng=norm_g[i][None, :], qg=q_norm_g[i][None, :], kvg=kv_norm_g[i][None, :],
                           gn=ret_gn_g[i][None, :], w_qkv=w_qkv, w_gate=w_gate, wuq=wuq, wukv=wukv,
                           wbr=_bf16(w_br_ret[i]), wbm=_bf16(w_br_mla[i]), wo=_bf16(w_out[i])))
    fg = final_norm_g[None, :]
    ret_tabs = _retention_tables()
    return (_trunk(x_prompt, layers, fg, ret_tabs), _trunk(x_sample, layers, fg, ret_tabs))
```

```python
import functools
import math

import jax
import jax.numpy as jnp
import numpy as np
from jax import lax
from jax.experimental import pallas as pl
from jax.experimental.pallas import tpu as pltpu

D_MODEL = 1024
DEPTH = 2
RET_HEADS = 4
RET_DK = 128
RET_DV = 128
RET_WIDTH = RET_HEADS * RET_DV
RET_LOG2_FWD = (-5.0, -6.0, -7.0, -8.0)
RET_LOG2_BWD = (-5.5, -6.5, -7.5, -8.5)
MLA_HEADS = 4
MLA_NOPE = 128
MLA_ROPE = 64
MLA_QK = MLA_NOPE + MLA_ROPE
MLA_V = 128
MLA_WIDTH = MLA_HEADS * MLA_V
Q_LORA = 256
KV_LORA = 128
ROPE_BASE = 10000.0
NORM_EPS = 1e-6

LANES = 128
MLA_QK_PAD = 2 * LANES
_OFF = np.cumsum((0, 512, 512, 512, 512, Q_LORA, KV_LORA, MLA_ROPE, MLA_WIDTH, D_MODEL, D_MODEL))
(O_RQ, O_RK, O_RV, O_RG, O_CQ, O_CKV, O_KPE, O_MG, O_GRET, O_GMLA, _) = [int(v) for v in _OFF]
QKV_COLS = 3 * RET_WIDTH + Q_LORA + KV_LORA + LANES
GATE_COLS = RET_WIDTH + MLA_WIDTH + 2 * D_MODEL

ROW_TILE = 512
RET_CHUNK = 256
RET_UNROLL = 8
ATT_Q_TILE = 1024
ATT_K_TILE = 512
ATT_ONES_ROWS = 16
VMEM_LIMIT = 56 * 1024 * 1024

_NT = (((1,), (1,)), ((), ()))
_TN = (((0,), (0,)), ((), ()))


def _bf16(x):
    return x.astype(jnp.bfloat16)


def _dot(a, b):
    return jnp.dot(a, b, preferred_element_type=jnp.float32)


def _rms(x, g):
    return x * lax.rsqrt(jnp.mean(x * x, axis=-1, keepdims=True) + NORM_EPS) * g


def _resident(shape):
    return pl.BlockSpec(shape, lambda *_: (0,) * len(shape))


def _in_proj_kernel(x_ref, ng_ref, w_ref, qg_ref, kvg_ref, wuq_ref, wukv_ref,
                    cosr_ref, sinr_ref, cosp_ref, sinlo_ref, sinhi_ref,
                    rq_ref, rk_ref, rv_ref, q_ref, kn_ref, v_ref, kpe_ref):
    h = _bf16(_rms(x_ref[...], ng_ref[...]))
    z = _dot(h, w_ref[...])

    cosr, sinr = cosr_ref[...], sinr_ref[...]
    k_scale = RET_DK ** -0.5
    for hd in range(RET_HEADS):
        sl = slice(hd * RET_DK, (hd + 1) * RET_DK)
        xq = z[:, sl]
        rq_ref[:, sl] = _bf16(xq * cosr + pltpu.roll(xq, RET_DK // 2, 1) * sinr)
        xk = z[:, RET_WIDTH + hd * RET_DK: RET_WIDTH + (hd + 1) * RET_DK]
        rk_ref[:, sl] = _bf16((xk * cosr + pltpu.roll(xk, RET_DK // 2, 1) * sinr) * k_scale)
    rv_ref[...] = _bf16(z[:, 2 * RET_WIDTH: 3 * RET_WIDTH])

    cosp, sinlo, sinhi = cosp_ref[...], sinlo_ref[...], sinhi_ref[...]

    def rope64(pe):
        return (pe * cosp + pltpu.roll(pe, LANES - MLA_ROPE // 2, 1) * sinlo
                + pltpu.roll(pe, MLA_ROPE // 2, 1) * sinhi)

    c0 = 3 * RET_WIDTH
    cq = _bf16(_rms(z[:, c0: c0 + Q_LORA], qg_ref[...]))
    q = _dot(cq, wuq_ref[...])
    q_scale = MLA_QK ** -0.5 * math.log2(math.e)
    for hd in range(MLA_HEADS):
        b0 = hd * MLA_QK_PAD
        q_ref[:, b0: b0 + LANES] = _bf16(q[:, b0: b0 + LANES] * q_scale)
        q_ref[:, b0 + LANES: b0 + 2 * LANES] = _bf16(rope64(q[:, b0 + LANES: b0 + 2 * LANES]) * q_scale)

    c1 = c0 + Q_LORA
    ckv = _bf16(_rms(z[:, c1: c1 + KV_LORA], kvg_ref[...]))
    kv = _dot(ckv, wukv_ref[...])
    kn_ref[...] = _bf16(kv[:, :MLA_HEADS * MLA_NOPE])
    v_ref[...] = _bf16(kv[:, MLA_HEADS * MLA_NOPE:])
    kpe_ref[...] = _bf16(rope64(z[:, c1 + KV_LORA: c1 + KV_LORA + LANES]))


def _in_proj(x2d, seq, ng, w_qkv, qg, kvg, wuq, wukv, tabs):
    tokens = x2d.shape[0]
    tm = ROW_TILE
    pos_tiles = seq // tm
    row = lambda width: pl.BlockSpec((tm, width), lambda i: (i, 0))
    pos = pl.BlockSpec((tm, LANES), lambda i: (i % pos_tiles, 0))
    bf = lambda width: jax.ShapeDtypeStruct((tokens, width), jnp.bfloat16)
    return pl.pallas_call(
        _in_proj_kernel,
        grid=(tokens // tm,),
        in_specs=[row(D_MODEL), _resident((1, D_MODEL)), _resident(w_qkv.shape),
                  _resident((1, Q_LORA)), _resident((1, KV_LORA)),
                  _resident(wuq.shape), _resident(wukv.shape), pos, pos, pos, pos, pos],
        out_specs=[row(RET_WIDTH), row(RET_WIDTH), row(RET_WIDTH), row(MLA_HEADS * MLA_QK_PAD),
                   row(MLA_HEADS * MLA_NOPE), row(MLA_WIDTH), row(LANES)],
        out_shape=[bf(RET_WIDTH), bf(RET_WIDTH), bf(RET_WIDTH), bf(MLA_HEADS * MLA_QK_PAD),
                   bf(MLA_HEADS * MLA_NOPE), bf(MLA_WIDTH), bf(LANES)],
        compiler_params=pltpu.CompilerParams(dimension_semantics=("parallel",),
                                             vmem_limit_bytes=VMEM_LIMIT),
        name="in_proj",
    )(x2d, ng, w_qkv, qg, kvg, wuq, wukv, *tabs)


def _retention_kernel(q_ref, k_ref, v_ref, dmask_ref, qdec_ref, kdec_ref, sdec_ref, gn_ref,
                      o_ref, local_ref, state_ref):
    c = RET_CHUNK
    n = q_ref.shape[0] // c
    kdec = kdec_ref[...]

    def local_kv(i, carry):
        rows = pl.ds(pl.multiple_of(i * c, c), c)
        kf = k_ref[rows, :].astype(jnp.float32)
        kz = jnp.concatenate([_bf16(kf * kdec[:, :RET_DK]), _bf16(kf * kdec[:, RET_DK:])], axis=1)
        local_ref[i] = lax.dot_general(kz, v_ref[rows, :], _TN,
                                       preferred_element_type=jnp.float32)
        return carry

    lax.fori_loop(0, n, local_kv, 0, unroll=RET_UNROLL)

    sdec = sdec_ref[...]

    def scan_fwd(i, st):
        state_ref[i, :RET_DK, :] = _bf16(st)
        return st * sdec[:RET_DK] + local_ref[i, :RET_DK, :]

    def scan_bwd(j, st):
        i = n - 1 - j
        state_ref[i, RET_DK:, :] = _bf16(st)
        return st * sdec[RET_DK:] + local_ref[i, RET_DK:, :]

    zero = jnp.zeros((RET_DK, RET_DV), jnp.float32)
    lax.fori_loop(0, n, scan_fwd, zero)
    lax.fori_loop(0, n, scan_bwd, zero)

    dmask = dmask_ref[...]
    qdec = qdec_ref[...]
    gain = gn_ref[...]

    def outputs(i, carry):
        rows = pl.ds(pl.multiple_of(i * c, c), c)
        qc, kc, vc = q_ref[rows, :], k_ref[rows, :], v_ref[rows, :]
        s = lax.dot_general(qc, kc, _NT, preferred_element_type=jnp.float32) * dmask
        qf = qc.astype(jnp.float32)
        qx = jnp.concatenate([_bf16(qf * qdec[:, :RET_DK]), _bf16(qf * qdec[:, RET_DK:])], axis=1)
        o = _dot(_bf16(s), vc) + _dot(qx, state_ref[i])
        mu = jnp.mean(o, axis=-1, keepdims=True)
        d = o - mu
        var = jnp.mean(d * d, axis=-1, keepdims=True)
        o_ref[rows, :] = _bf16(d * lax.rsqrt(var + NORM_EPS) * gain)
        return carry

    lax.fori_loop(0, n, outputs, 0, unroll=RET_UNROLL)


def _retention(rq, rk, rv, gn, dec):
    b, seq, _ = rq.shape
    c = RET_CHUNK
    n = seq // c
    head = pl.BlockSpec((None, seq, RET_DK), lambda i, h: (i, 0, h))
    per_head = lambda r, w: pl.BlockSpec((None, r, w), lambda i, h: (h, 0, 0))
    return pl.pallas_call(
        _retention_kernel,
        grid=(b, RET_HEADS),
        in_specs=[head, head, head, per_head(c, c), per_head(c, 2 * RET_DK), per_head(c, 2 * RET_DK),
                  per_head(2 * RET_DK, RET_DV), pl.BlockSpec((1, RET_DV), lambda i, h: (0, h))],
        out_specs=head,
        out_shape=jax.ShapeDtypeStruct((b, seq, RET_WIDTH), jnp.bfloat16),
        scratch_shapes=[pltpu.VMEM((n, 2 * RET_DK, RET_DV), jnp.float32),
                        pltpu.VMEM((n, 2 * RET_DK, RET_DV), jnp.bfloat16)],
        compiler_params=pltpu.CompilerParams(dimension_semantics=("parallel", "arbitrary"),
                                             vmem_limit_bytes=VMEM_LIMIT),
        name="retention",
    )(rq, rk, rv, *dec, gn)


def _retention_tables():
    c = RET_CHUNK
    lg_f = jnp.log1p(-jnp.exp2(jnp.array(RET_LOG2_FWD, jnp.float32)))[:, None, None]
    lg_b = jnp.log1p(-jnp.exp2(jnp.array(RET_LOG2_BWD, jnp.float32)))[:, None, None]
    pos = jnp.arange(c, dtype=jnp.float32)
    diff = pos[:, None] - pos[None, :]
    dmask = jnp.where(diff >= 0, jnp.exp(lg_f * jnp.maximum(diff, 0.0)),
                      jnp.exp(lg_b * jnp.maximum(-diff, 0.0)))
    col = pos[None, :, None]
    wide = lambda f, b_: jnp.concatenate([jnp.broadcast_to(f, (RET_HEADS, c, RET_DK)),
                                          jnp.broadcast_to(b_, (RET_HEADS, c, RET_DK))], axis=2)
    qdec = wide(jnp.exp(lg_f * (col + 1.0)), jnp.exp(lg_b * (c - col)))
    kdec = wide(jnp.exp(lg_f * (c - 1.0 - col)), jnp.exp(lg_b * col))
    sdec = jnp.concatenate([jnp.broadcast_to(jnp.exp(lg_f * c), (RET_HEADS, RET_DK, RET_DV)),
                            jnp.broadcast_to(jnp.exp(lg_b * c), (RET_HEADS, RET_DK, RET_DV))], axis=1)
    return dmask, qdec, kdec, sdec


def _attention_kernel(q_ref, kn_ref, kpe_ref, v_ref, o_ref, kcat_ref, vt_ref):
    seq = q_ref.shape[0]
    tq, tk = ATT_Q_TILE, ATT_K_TILE
    kcat_ref[:, :LANES] = kn_ref[...]
    kcat_ref[:, LANES:] = kpe_ref[...]
    for j in range(seq // tk):
        vt_ref[j, :MLA_V, :] = _bf16(v_ref[j * tk:(j + 1) * tk, :].astype(jnp.float32).T)
        vt_ref[j, MLA_V:, :] = jnp.ones((ATT_ONES_ROWS, tk), jnp.bfloat16)

    def q_tile(i, carry):
        qrows = pl.ds(pl.multiple_of(i * tq, tq), tq)
        qt = _bf16(q_ref[qrows, :].astype(jnp.float32).T)

        def k_tile(j, st):
            m, acc = st
            krows = pl.ds(pl.multiple_of(j * tk, tk), tk)
            s = _dot(kcat_ref[krows, :], qt)
            m_new = jnp.maximum(m, jnp.max(s, axis=0, keepdims=True))
            alpha = jnp.exp2(m - m_new)
            p = _bf16(jnp.exp2(s - m_new))
            return m_new, alpha * acc + _dot(vt_ref[j], p)

        init = (jnp.full((1, tq), -jnp.inf, jnp.float32),
                jnp.zeros((MLA_V + ATT_ONES_ROWS, tq), jnp.float32))
        _, acc = lax.fori_loop(0, seq // tk, k_tile, init, unroll=True)
        o_ref[qrows, :] = _bf16((acc[:MLA_V] / acc[MLA_V:MLA_V + 1]).T)
        return carry

    lax.fori_loop(0, seq // tq, q_tile, 0)


def _attention(q, kn, kpe, v):
    b, seq, _ = q.shape
    tk = ATT_K_TILE
    head = lambda w: pl.BlockSpec((None, seq, w), lambda i, h: (i, 0, h))
    return pl.pallas_call(
        _attention_kernel,
        grid=(b, MLA_HEADS),
        in_specs=[head(MLA_QK_PAD), head(MLA_NOPE),
                  pl.BlockSpec((None, seq, LANES), lambda i, h: (i, 0, 0)), head(MLA_V)],
        out_specs=head(MLA_V),
        out_shape=jax.ShapeDtypeStruct((b, seq, MLA_WIDTH), jnp.bfloat16),
        scratch_shapes=[pltpu.VMEM((seq, MLA_QK_PAD), jnp.bfloat16),
                        pltpu.VMEM((seq // tk, MLA_V + ATT_ONES_ROWS, tk), jnp.bfloat16)],
        compiler_params=pltpu.CompilerParams(dimension_semantics=("parallel", "arbitrary"),
                                             vmem_limit_bytes=VMEM_LIMIT),
        name="attention",
    )(q, kn, kpe, v)


def _out_proj_kernel(x_ref, ret_ref, att_ref, ng_ref, wg_ref, wbr_ref, wbm_ref, wo_ref, fg_ref,
                     y_ref, *, final_norm):
    x = x_ref[...]
    h = _bf16(_rms(x, ng_ref[...]))
    g = _dot(h, wg_ref[...])
    rg = g[:, :RET_WIDTH]
    mg = g[:, RET_WIDTH: RET_WIDTH + MLA_WIDTH]
    g_ret = g[:, RET_WIDTH + MLA_WIDTH: RET_WIDTH + MLA_WIDTH + D_MODEL]
    g_mla = g[:, RET_WIDTH + MLA_WIDTH + D_MODEL:]
    ret_out = _dot(_bf16(ret_ref[...].astype(jnp.float32) * (rg * jax.nn.sigmoid(rg))), wbr_ref[...])
    mla_out = _dot(_bf16(att_ref[...].astype(jnp.float32) * (mg * jax.nn.sigmoid(mg))), wbm_ref[...])
    merged = jax.nn.sigmoid(g_ret) * ret_out + jax.nn.sigmoid(g_mla) * mla_out
    y = x + _dot(_bf16(merged), wo_ref[...])
    if final_norm:
        y = _rms(y, fg_ref[...])
    y_ref[...] = y


def _out_proj(x2d, ret, att, ng, w_gate, w_br_ret, w_br_mla, w_out, fg, final_norm):
    tokens = x2d.shape[0]
    tm = ROW_TILE
    row = lambda width: pl.BlockSpec((tm, width), lambda i: (i, 0))
    return pl.pallas_call(
        functools.partial(_out_proj_kernel, final_norm=final_norm),
        grid=(tokens // tm,),
        in_specs=[row(D_MODEL), row(RET_WIDTH), row(MLA_WIDTH), _resident((1, D_MODEL)),
                  _resident(w_gate.shape), _resident(w_br_ret.shape), _resident(w_br_mla.shape),
                  _resident(w_out.shape), _resident((1, D_MODEL))],
        out_specs=row(D_MODEL),
        out_shape=jax.ShapeDtypeStruct((tokens, D_MODEL), jnp.float32),
        compiler_params=pltpu.CompilerParams(dimension_semantics=("parallel",),
                                             vmem_limit_bytes=VMEM_LIMIT),
        name="out_proj",
    )(x2d, ret, att, ng, w_gate, w_br_ret, w_br_mla, w_out, fg)


def _rope_tables(seq):
    posn = jnp.arange(seq, dtype=jnp.float32)[:, None]

    def angles(d):
        inv = ROPE_BASE ** (-jnp.arange(0, d, 2, dtype=jnp.float32) / d)
        return posn * inv[None, :]

    ar = angles(RET_DK)
    cos_r = jnp.concatenate([jnp.cos(ar), jnp.cos(ar)], axis=1)
    sin_r = jnp.concatenate([-jnp.sin(ar), jnp.sin(ar)], axis=1)
    ap = angles(MLA_ROPE)
    z32 = jnp.zeros_like(ap)
    z64 = jnp.zeros((seq, LANES - MLA_ROPE), jnp.float32)
    cos_p = jnp.concatenate([jnp.cos(ap), jnp.cos(ap), z64], axis=1)
    sin_lo = jnp.concatenate([-jnp.sin(ap), z32, z64], axis=1)
    sin_hi = jnp.concatenate([z32, jnp.sin(ap), z64], axis=1)
    return cos_r, sin_r, cos_p, sin_lo, sin_hi


def _prep_layer(w_in, w_uq, w_ukv):
    zpad = jnp.zeros((D_MODEL, LANES - MLA_ROPE), w_in.dtype)
    w_qkv = jnp.concatenate([w_in[:, O_RQ:O_RG], w_in[:, O_CQ:O_MG], zpad], axis=1)
    w_gate = jnp.concatenate([w_in[:, O_RG:O_CQ], w_in[:, O_MG:]], axis=1)
    uq = w_uq.reshape(Q_LORA, MLA_HEADS, MLA_QK)
    uq = jnp.concatenate([uq, jnp.zeros((Q_LORA, MLA_HEADS, MLA_QK_PAD - MLA_QK), w_uq.dtype)], axis=2)
    ukv = w_ukv.reshape(KV_LORA, MLA_HEADS, MLA_NOPE + MLA_V)
    ukv = jnp.concatenate([ukv[:, :, :MLA_NOPE].reshape(KV_LORA, -1),
                           ukv[:, :, MLA_NOPE:].reshape(KV_LORA, -1)], axis=1)
    return _bf16(w_qkv), _bf16(w_gate), _bf16(uq.reshape(Q_LORA, -1)), _bf16(ukv)


def _trunk(x, layers, final_norm_g, ret_tabs):
    b, seq, _ = x.shape
    rope_tabs = _rope_tables(seq)
    x2d = x.reshape(b * seq, D_MODEL)
    for i, lw in enumerate(layers):
        rq, rk, rv, q, kn, v, kpe = _in_proj(x2d, seq, lw["ng"], lw["w_qkv"], lw["qg"], lw["kvg"],
                                             lw["wuq"], lw["wukv"], rope_tabs)
        sh = lambda t: t.reshape(b, seq, t.shape[-1])
        ret = _retention(sh(rq), sh(rk), sh(rv), lw["gn"], ret_tabs)
        att = _attention(sh(q), sh(kn), sh(kpe), sh(v))
        x2d = _out_proj(x2d, ret.reshape(b * seq, -1), att.reshape(b * seq, -1), lw["ng"], lw["w_gate"],
                        lw["wbr"], lw["wbm"], lw["wo"], final_norm_g, final_norm=(i == DEPTH - 1))
    return x2d.reshape(b, seq, D_MODEL)


def kernel(x_prompt, x_sample, norm_g, w_in, ret_gn_g, q_norm_g, kv_norm_g, w_uq, w_ukv, w_br_ret, w_br_mla,
           w_out, final_norm_g):
    layers = []
    for i in range(DEPTH):
        w_qkv, w_gate, wuq, wukv = _prep_layer(w_in[i], w_uq[i], w_ukv[i])
        layers.append(dict(ng=norm_g[i][None, :], qg=q_norm_g[i][None, :], kvg=kv_norm_g[i][None, :],
                           gn=ret_gn_g[i][None, :], w_qkv=w_qkv, w_gate=w_gate, wuq=wuq, wukv=wukv,
                           wbr=_bf16(w_br_ret[i]), wbm=_bf16(w_br_mla[i]), wo=_bf16(w_out[i])))
    fg = final_norm_g[None, :]
    ret_tabs = _retention_tables()
    return (_trunk(x_prompt, layers, fg, ret_tabs), _trunk(x_sample, layers, fg, ret_tabs))
```

```python
import functools
import math

import jax
import jax.numpy as jnp
import numpy as np
from jax import lax
from jax.experimental import pallas as pl
from jax.experimental.pallas import tpu as pltpu

D_MODEL = 1024
DEPTH = 2
RET_HEADS = 4
RET_DK = 128
RET_DV = 128
RET_WIDTH = RET_HEADS * RET_DV
RET_LOG2_FWD = (-5.0, -6.0, -7.0, -8.0)
RET_LOG2_BWD = (-5.5, -6.5, -7.5, -8.5)
MLA_HEADS = 4
MLA_NOPE = 128
MLA_ROPE = 64
MLA_QK = MLA_NOPE + MLA_ROPE
MLA_V = 128
MLA_WIDTH = MLA_HEADS * MLA_V
Q_LORA = 256
KV_LORA = 128
ROPE_BASE = 10000.0
NORM_EPS = 1e-6

LANES = 128
MLA_QK_PAD = 2 * LANES
_OFF = np.cumsum((0, 512, 512, 512, 512, Q_LORA, KV_LORA, MLA_ROPE, MLA_WIDTH, D_MODEL, D_MODEL))
(O_RQ, O_RK, O_RV, O_RG, O_CQ, O_CKV, O_KPE, O_MG, O_GRET, O_GMLA, _) = [int(v) for v in _OFF]

ROW_TILE = 512
RET_CHUNK = 256
RET_UNROLL = 8
ATT_Q_TILE = 2048
ATT_K_TILE = 512
ATT_ONES_ROWS = 16
VMEM_LIMIT = 56 * 1024 * 1024

_NT = (((1,), (1,)), ((), ()))
_TN = (((0,), (0,)), ((), ()))


def _bf16(x):
    return x.astype(jnp.bfloat16)


def _dot(a, b):
    return jnp.dot(a, b, preferred_element_type=jnp.float32)


def _rms(x, g):
    return x * lax.rsqrt(jnp.mean(x * x, axis=-1, keepdims=True) + NORM_EPS) * g


def _resident(shape):
    return pl.BlockSpec(shape, lambda *_: (0,) * len(shape))


def _in_proj_kernel(x_ref, ng_ref, wlat_ref, wret_ref, qg_ref, kvg_ref, wuq_ref, wukv_ref,
                    cosr_ref, sinr_ref, cosp_ref, sinlo_ref, sinhi_ref,
                    rq_ref, rk_ref, rv_ref, q_ref, kn_ref, v_ref, kpe_ref):
    h = _bf16(_rms(x_ref[...], ng_ref[...]))
    zl = _dot(h, wlat_ref[...])
    zr = _dot(h, wret_ref[...])

    cq = _bf16(_rms(zl[:, :Q_LORA], qg_ref[...]))
    ckv = _bf16(_rms(zl[:, Q_LORA: Q_LORA + KV_LORA], kvg_ref[...]))
    q = _dot(cq, wuq_ref[...])
    kv = _dot(ckv, wukv_ref[...])

    cosr, sinr = cosr_ref[...], sinr_ref[...]
    k_scale = RET_DK ** -0.5
    for hd in range(RET_HEADS):
        sl = slice(hd * RET_DK, (hd + 1) * RET_DK)
        xq = zr[:, sl]
        rq_ref[:, sl] = _bf16(xq * cosr + pltpu.roll(xq, RET_DK // 2, 1) * sinr)
        xk = zr[:, RET_WIDTH + hd * RET_DK: RET_WIDTH + (hd + 1) * RET_DK]
        rk_ref[:, sl] = _bf16((xk * cosr + pltpu.roll(xk, RET_DK // 2, 1) * sinr) * k_scale)
    rv_ref[...] = _bf16(zr[:, 2 * RET_WIDTH:])

    cosp, sinlo, sinhi = cosp_ref[...], sinlo_ref[...], sinhi_ref[...]

    def rope64(pe):
        return (pe * cosp + pltpu.roll(pe, LANES - MLA_ROPE // 2, 1) * sinlo
                + pltpu.roll(pe, MLA_ROPE // 2, 1) * sinhi)

    kpe_ref[...] = _bf16(rope64(zl[:, Q_LORA + KV_LORA:]))
    kn_ref[...] = _bf16(kv[:, :MLA_HEADS * MLA_NOPE])
    v_ref[...] = _bf16(kv[:, MLA_HEADS * MLA_NOPE:])
    for hd in range(MLA_HEADS):
        b0 = hd * MLA_QK_PAD
        q_ref[:, b0: b0 + LANES] = _bf16(q[:, b0: b0 + LANES])
        q_ref[:, b0 + LANES: b0 + 2 * LANES] = _bf16(rope64(q[:, b0 + LANES: b0 + 2 * LANES]))


def _in_proj(x2d, seq, ng, w_lat, w_ret, qg, kvg, wuq, wukv, tabs):
    tokens = x2d.shape[0]
    tm = ROW_TILE
    pos_tiles = seq // tm
    row = lambda width: pl.BlockSpec((tm, width), lambda i: (i, 0))
    pos = pl.BlockSpec((tm, LANES), lambda i: (i % pos_tiles, 0))
    bf = lambda width: jax.ShapeDtypeStruct((tokens, width), jnp.bfloat16)
    return pl.pallas_call(
        _in_proj_kernel,
        grid=(tokens // tm,),
        in_specs=[row(D_MODEL), _resident((1, D_MODEL)), _resident(w_lat.shape), _resident(w_ret.shape),
                  _resident((1, Q_LORA)), _resident((1, KV_LORA)),
                  _resident(wuq.shape), _resident(wukv.shape), pos, pos, pos, pos, pos],
        out_specs=[row(RET_WIDTH), row(RET_WIDTH), row(RET_WIDTH), row(MLA_HEADS * MLA_QK_PAD),
                   row(MLA_HEADS * MLA_NOPE), row(MLA_WIDTH), row(LANES)],
        out_shape=[bf(RET_WIDTH), bf(RET_WIDTH), bf(RET_WIDTH), bf(MLA_HEADS * MLA_QK_PAD),
                   bf(MLA_HEADS * MLA_NOPE), bf(MLA_WIDTH), bf(LANES)],
        compiler_params=pltpu.CompilerParams(dimension_semantics=("parallel",),
                                             vmem_limit_bytes=VMEM_LIMIT),
        name="in_proj",
    )(x2d, ng, w_lat, w_ret, qg, kvg, wuq, wukv, *tabs)


def _retention_kernel(q_ref, k_ref, v_ref, dmask_ref, qdec_ref, kdec_ref, sdec_ref, gn_ref,
                      o_ref, local_ref, state_ref):
    c = RET_CHUNK
    n = q_ref.shape[0] // c
    kdec = kdec_ref[...]

    def local_kv(i, carry):
        rows = pl.ds(pl.multiple_of(i * c, c), c)
        kf = k_ref[rows, :].astype(jnp.float32)
        kz = jnp.concatenate([_bf16(kf * kdec[:, :RET_DK]), _bf16(kf * kdec[:, RET_DK:])], axis=1)
        local_ref[i] = lax.dot_general(kz, v_ref[rows, :], _TN,
                                       preferred_element_type=jnp.float32)
        return carry

    lax.fori_loop(0, n, local_kv, 0, unroll=RET_UNROLL)

    sdec = sdec_ref[...]

    def scan_fwd(i, st):
        state_ref[i, :RET_DK, :] = _bf16(st)
        return st * sdec[:RET_DK] + local_ref[i, :RET_DK, :]

    def scan_bwd(j, st):
        i = n - 1 - j
        state_ref[i, RET_DK:, :] = _bf16(st)
        return st * sdec[RET_DK:] + local_ref[i, RET_DK:, :]

    zero = jnp.zeros((RET_DK, RET_DV), jnp.float32)
    lax.fori_loop(0, n, scan_fwd, zero)
    lax.fori_loop(0, n, scan_bwd, zero)

    dmask = dmask_ref[...]
    qdec = qdec_ref[...]
    gain = gn_ref[...]

    def outputs(i, carry):
        rows = pl.ds(pl.multiple_of(i * c, c), c)
        qc, kc, vc = q_ref[rows, :], k_ref[rows, :], v_ref[rows, :]
        s = lax.dot_general(qc, kc, _NT, preferred_element_type=jnp.float32) * dmask
        qf = qc.astype(jnp.float32)
        qx = jnp.concatenate([_bf16(qf * qdec[:, :RET_DK]), _bf16(qf * qdec[:, RET_DK:])], axis=1)
        o = _dot(_bf16(s), vc) + _dot(qx, state_ref[i])
        mu = jnp.mean(o, axis=-1, keepdims=True)
        d = o - mu
        var = jnp.mean(d * d, axis=-1, keepdims=True)
        o_ref[rows, :] = _bf16(d * lax.rsqrt(var + NORM_EPS) * gain)
        return carry

    lax.fori_loop(0, n, outputs, 0, unroll=RET_UNROLL)


def _retention(rq, rk, rv, gn, dec):
    b, seq, _ = rq.shape
    c = RET_CHUNK
    n = seq // c
    head = pl.BlockSpec((None, seq, RET_DK), lambda i, h: (i, 0, h))
    per_head = lambda r, w: pl.BlockSpec((None, r, w), lambda i, h: (h, 0, 0))
    return pl.pallas_call(
        _retention_kernel,
        grid=(b, RET_HEADS),
        in_specs=[head, head, head, per_head(c, c), per_head(c, 2 * RET_DK), per_head(c, 2 * RET_DK),
                  per_head(2 * RET_DK, RET_DV), pl.BlockSpec((1, RET_DV), lambda i, h: (0, h))],
        out_specs=head,
        out_shape=jax.ShapeDtypeStruct((b, seq, RET_WIDTH), jnp.bfloat16),
        scratch_shapes=[pltpu.VMEM((n, 2 * RET_DK, RET_DV), jnp.float32),
                        pltpu.VMEM((n, 2 * RET_DK, RET_DV), jnp.bfloat16)],
        compiler_params=pltpu.CompilerParams(dimension_semantics=("parallel", "arbitrary"),
                                             vmem_limit_bytes=VMEM_LIMIT),
        name="retention",
    )(rq, rk, rv, *dec, gn)


def _retention_tables():
    c = RET_CHUNK
    lg_f = jnp.log1p(-jnp.exp2(jnp.array(RET_LOG2_FWD, jnp.float32)))[:, None, None]
    lg_b = jnp.log1p(-jnp.exp2(jnp.array(RET_LOG2_BWD, jnp.float32)))[:, None, None]
    pos = jnp.arange(c, dtype=jnp.float32)
    diff = pos[:, None] - pos[None, :]
    dmask = jnp.where(diff >= 0, jnp.exp(lg_f * jnp.maximum(diff, 0.0)),
                      jnp.exp(lg_b * jnp.maximum(-diff, 0.0)))
    col = pos[None, :, None]
    wide = lambda f, b_: jnp.concatenate([jnp.broadcast_to(f, (RET_HEADS, c, RET_DK)),
                                          jnp.broadcast_to(b_, (RET_HEADS, c, RET_DK))], axis=2)
    qdec = wide(jnp.exp(lg_f * (col + 1.0)), jnp.exp(lg_b * (c - col)))
    kdec = wide(jnp.exp(lg_f * (c - 1.0 - col)), jnp.exp(lg_b * col))
    sdec = jnp.concatenate([jnp.broadcast_to(jnp.exp(lg_f * c), (RET_HEADS, RET_DK, RET_DV)),
                            jnp.broadcast_to(jnp.exp(lg_b * c), (RET_HEADS, RET_DK, RET_DV))], axis=1)
    return dmask, qdec, kdec, sdec


def _attention_kernel(q_ref, kn_ref, kpe_ref, v_ref, o_ref, kcat_ref, vt_ref):
    seq = q_ref.shape[0]
    tq, tk = min(ATT_Q_TILE, seq), ATT_K_TILE
    kcat_ref[:, :LANES] = kn_ref[...]
    kcat_ref[:, LANES:] = kpe_ref[...]
    for j in range(seq // tk):
        vt_ref[j, :MLA_V, :] = _bf16(v_ref[j * tk:(j + 1) * tk, :].astype(jnp.float32).T)
        vt_ref[j, MLA_V:, :] = jnp.ones((ATT_ONES_ROWS, tk), jnp.bfloat16)

    def q_tile(i, carry):
        qrows = pl.ds(pl.multiple_of(i * tq, tq), tq)
        qt = _bf16(q_ref[qrows, :].astype(jnp.float32).T)

        def scores(j):
            return _dot(kcat_ref[j * tk:(j + 1) * tk, :], qt)

        m = jnp.full((1, tq), -jnp.inf, jnp.float32)
        acc = jnp.zeros((MLA_V + ATT_ONES_ROWS, tq), jnp.float32)
        n_k = seq // tk
        s = scores(0)
        for j in range(n_k):
            s_next = scores(j + 1) if j + 1 < n_k else None
            m_new = jnp.maximum(m, jnp.max(s, axis=0, keepdims=True))
            alpha = jnp.exp2(m - m_new)
            p = _bf16(jnp.exp2(s - m_new))
            acc = alpha * acc + _dot(vt_ref[j], p)
            m, s = m_new, s_next
        o_ref[qrows, :] = _bf16((acc[:MLA_V] / acc[MLA_V:MLA_V + 1]).T)
        return carry

    lax.fori_loop(0, seq // tq, q_tile, 0)


def _attention(q, kn, kpe, v):
    b, seq, _ = q.shape
    tk = ATT_K_TILE
    head = lambda w: pl.BlockSpec((None, seq, w), lambda i, h: (i, 0, h))
    return pl.pallas_call(
        _attention_kernel,
        grid=(b, MLA_HEADS),
        in_specs=[head(MLA_QK_PAD), head(MLA_NOPE),
                  pl.BlockSpec((None, seq, LANES), lambda i, h: (i, 0, 0)), head(MLA_V)],
        out_specs=head(MLA_V),
        out_shape=jax.ShapeDtypeStruct((b, seq, MLA_WIDTH), jnp.bfloat16),
        scratch_shapes=[pltpu.VMEM((seq, MLA_QK_PAD), jnp.bfloat16),
                        pltpu.VMEM((seq // tk, MLA_V + ATT_ONES_ROWS, tk), jnp.bfloat16)],
        compiler_params=pltpu.CompilerParams(dimension_semantics=("parallel", "arbitrary"),
                                             vmem_limit_bytes=VMEM_LIMIT),
        name="attention",
    )(q, kn, kpe, v)


def _out_proj_kernel(x_ref, ret_ref, att_ref, ng_ref, wg_ref, wbr_ref, wbm_ref, wo_ref, fg_ref,
                     y_ref, *, final_norm):
    x = x_ref[...]
    h = _bf16(_rms(x, ng_ref[...]))
    g = _dot(h, wg_ref[...])
    rg = g[:, :RET_WIDTH]
    mg = g[:, RET_WIDTH: RET_WIDTH + MLA_WIDTH]
    g_ret = g[:, RET_WIDTH + MLA_WIDTH: RET_WIDTH + MLA_WIDTH + D_MODEL]
    g_mla = g[:, RET_WIDTH + MLA_WIDTH + D_MODEL:]
    ret_out = _dot(_bf16(ret_ref[...].astype(jnp.float32) * (rg * jax.nn.sigmoid(rg))), wbr_ref[...])
    mla_out = _dot(_bf16(att_ref[...].astype(jnp.float32) * (mg * jax.nn.sigmoid(mg))), wbm_ref[...])
    merged = jax.nn.sigmoid(g_ret) * ret_out + jax.nn.sigmoid(g_mla) * mla_out
    y = x + _dot(_bf16(merged), wo_ref[...])
    if final_norm:
        y = _rms(y, fg_ref[...])
    y_ref[...] = y


def _out_proj(x2d, ret, att, ng, w_gate, w_br_ret, w_br_mla, w_out, fg, final_norm):
    tokens = x2d.shape[0]
    tm = ROW_TILE
    row = lambda width: pl.BlockSpec((tm, width), lambda i: (i, 0))
    return pl.pallas_call(
        functools.partial(_out_proj_kernel, final_norm=final_norm),
        grid=(tokens // tm,),
        in_specs=[row(D_MODEL), row(RET_WIDTH), row(MLA_WIDTH), _resident((1, D_MODEL)),
                  _resident(w_gate.shape), _resident(w_br_ret.shape), _resident(w_br_mla.shape),
                  _resident(w_out.shape), _resident((1, D_MODEL))],
        out_specs=row(D_MODEL),
        out_shape=jax.ShapeDtypeStruct((tokens, D_MODEL), jnp.float32),
        compiler_params=pltpu.CompilerParams(dimension_semantics=("parallel",),
                                             vmem_limit_bytes=VMEM_LIMIT),
        name="out_proj",
    )(x2d, ret, att, ng, w_gate, w_br_ret, w_br_mla, w_out, fg)


def _rope_tables(seq):
    posn = jnp.arange(seq, dtype=jnp.float32)[:, None]

    def angles(d):
        inv = ROPE_BASE ** (-jnp.arange(0, d, 2, dtype=jnp.float32) / d)
        return posn * inv[None, :]

    ar = angles(RET_DK)
    cos_r = jnp.concatenate([jnp.cos(ar), jnp.cos(ar)], axis=1)
    sin_r = jnp.concatenate([-jnp.sin(ar), jnp.sin(ar)], axis=1)
    ap = angles(MLA_ROPE)
    z32 = jnp.zeros_like(ap)
    z64 = jnp.zeros((seq, LANES - MLA_ROPE), jnp.float32)
    cos_p = jnp.concatenate([jnp.cos(ap), jnp.cos(ap), z64], axis=1)
    sin_lo = jnp.concatenate([-jnp.sin(ap), z32, z64], axis=1)
    sin_hi = jnp.concatenate([z32, jnp.sin(ap), z64], axis=1)
    return cos_r, sin_r, cos_p, sin_lo, sin_hi


def _prep_layer(w_in, w_uq, w_ukv):
    zpad = jnp.zeros((D_MODEL, LANES - MLA_ROPE), w_in.dtype)
    w_lat = jnp.concatenate([w_in[:, O_CQ:O_MG], zpad], axis=1)
    w_ret = w_in[:, O_RQ:O_RG]
    w_gate = jnp.concatenate([w_in[:, O_RG:O_CQ], w_in[:, O_MG:]], axis=1)
    uq = w_uq.reshape(Q_LORA, MLA_HEADS, MLA_QK)
    uq = jnp.concatenate([uq, jnp.zeros((Q_LORA, MLA_HEADS, MLA_QK_PAD - MLA_QK), w_uq.dtype)], axis=2)
    ukv = w_ukv.reshape(KV_LORA, MLA_HEADS, MLA_NOPE + MLA_V)
    ukv = jnp.concatenate([ukv[:, :, :MLA_NOPE].reshape(KV_LORA, -1),
                           ukv[:, :, MLA_NOPE:].reshape(KV_LORA, -1)], axis=1)
    return _bf16(w_lat), _bf16(w_ret), _bf16(w_gate), _bf16(uq.reshape(Q_LORA, -1)), _bf16(ukv)


def _trunk(x, layers, final_norm_g, ret_tabs):
    b, seq, _ = x.shape
    rope_tabs = _rope_tables(seq)
    x2d = x.reshape(b * seq, D_MODEL)
    for i, lw in enumerate(layers):
        rq, rk, rv, q, kn, v, kpe = _in_proj(x2d, seq, lw["ng"], lw["w_lat"], lw["w_ret"], lw["qg"],
                                             lw["kvg"], lw["wuq"], lw["wukv"], rope_tabs)
        sh = lambda t: t.reshape(b, seq, t.shape[-1])
        ret = _retention(sh(rq), sh(rk), sh(rv), lw["gn"], ret_tabs)
        att = _attention(sh(q), sh(kn), sh(kpe), sh(v))
        x2d = _out_proj(x2d, ret.reshape(b * seq, -1), att.reshape(b * seq, -1), lw["ng"], lw["w_gate"],
                        lw["wbr"], lw["wbm"], lw["wo"], final_norm_g, final_norm=(i == DEPTH - 1))
    return x2d.reshape(b, seq, D_MODEL)


def kernel(x_prompt, x_sample, norm_g, w_in, ret_gn_g, q_norm_g, kv_norm_g, w_uq, w_ukv, w_br_ret, w_br_mla,
           w_out, final_norm_g):
    layers = []
    for i in range(DEPTH):
        w_lat, w_ret, w_gate, wuq, wukv = _prep_layer(w_in[i], w_uq[i], w_ukv[i])
        qg = q_norm_g[i][None, :] * (MLA_QK ** -0.5 * math.log2(math.e))
        layers.append(dict(ng=norm_g[i][None, :], qg=qg, kvg=kv_norm_g[i][None, :],
                           gn=ret_gn_g[i][None, :], w_lat=w_lat, w_ret=w_ret, w_gate=w_gate, wuq=wuq, wukv=wukv,
                           wbr=_bf16(w_br_ret[i]), wbm=_bf16(w_br_mla[i]), wo=_bf16(w_out[i])))
    fg = final_norm_g[None, :]
    ret_tabs = _retention_tables()
    return (_trunk(x_prompt, layers, fg, ret_tabs), _trunk(x_sample, layers, fg, ret_tabs))
```

```python
import functools
import math

import jax
import jax.numpy as jnp
import numpy as np
from jax import lax
from jax.experimental import pallas as pl
from jax.experimental.pallas import tpu as pltpu

D_MODEL = 1024
DEPTH = 2
RET_HEADS = 4
RET_DK = 128
RET_DV = 128
RET_WIDTH = RET_HEADS * RET_DV
RET_LOG2_FWD = (-5.0, -6.0, -7.0, -8.0)
RET_LOG2_BWD = (-5.5, -6.5, -7.5, -8.5)
MLA_HEADS = 4
MLA_NOPE = 128
MLA_ROPE = 64
MLA_QK = MLA_NOPE + MLA_ROPE
MLA_V = 128
MLA_WIDTH = MLA_HEADS * MLA_V
Q_LORA = 256
KV_LORA = 128
ROPE_BASE = 10000.0
NORM_EPS = 1e-6

LANES = 128
MLA_QK_PAD = 2 * LANES
_OFF = np.cumsum((0, 512, 512, 512, 512, Q_LORA, KV_LORA, MLA_ROPE, MLA_WIDTH, D_MODEL, D_MODEL))
(O_RQ, O_RK, O_RV, O_RG, O_CQ, O_CKV, O_KPE, O_MG, O_GRET, O_GMLA, _) = [int(v) for v in _OFF]

ROW_TILE = 1024
RET_CHUNK = 256
ATT_Q_TILE = 2048
ATT_K_TILE = 512
ATT_ONES_ROWS = 16
VMEM_LIMIT = 56 * 1024 * 1024

_NT = (((1,), (1,)), ((), ()))
_TN = (((0,), (0,)), ((), ()))


def _bf16(x):
    return x.astype(jnp.bfloat16)


def _dot(a, b):
    return jnp.dot(a, b, preferred_element_type=jnp.float32)


def _rms(x, g):
    return x * lax.rsqrt(jnp.mean(x * x, axis=-1, keepdims=True) + NORM_EPS) * g


def _resident(shape):
    return pl.BlockSpec(shape, lambda *_: (0,) * len(shape))


def _in_proj_kernel(x_ref, ng_ref, wlat_ref, wret_ref, qg_ref, kvg_ref, wuq_ref, wukv_ref,
                    cosr_ref, sinr_ref, cosp_ref, sinlo_ref, sinhi_ref,
                    rq_ref, rk_ref, rv_ref, q_ref, kn_ref, v_ref, kpe_ref):
    h = _bf16(_rms(x_ref[...], ng_ref[...]))
    zl = _dot(h, wlat_ref[...])
    zr = _dot(h, wret_ref[...])

    cq = _bf16(_rms(zl[:, :Q_LORA], qg_ref[...]))
    ckv = _bf16(_rms(zl[:, Q_LORA: Q_LORA + KV_LORA], kvg_ref[...]))
    q = _dot(cq, wuq_ref[...])
    kv = _dot(ckv, wukv_ref[...])

    cosr, sinr = cosr_ref[...], sinr_ref[...]
    k_scale = RET_DK ** -0.5
    for hd in range(RET_HEADS):
        sl = slice(hd * RET_DK, (hd + 1) * RET_DK)
        xq = zr[:, sl]
        rq_ref[:, sl] = _bf16(xq * cosr + pltpu.roll(xq, RET_DK // 2, 1) * sinr)
        xk = zr[:, RET_WIDTH + hd * RET_DK: RET_WIDTH + (hd + 1) * RET_DK]
        rk_ref[:, sl] = _bf16((xk * cosr + pltpu.roll(xk, RET_DK // 2, 1) * sinr) * k_scale)
    rv_ref[...] = _bf16(zr[:, 2 * RET_WIDTH:])

    cosp, sinlo, sinhi = cosp_ref[...], sinlo_ref[...], sinhi_ref[...]

    def rope64(pe):
        return (pe * cosp + pltpu.roll(pe, LANES - MLA_ROPE // 2, 1) * sinlo
                + pltpu.roll(pe, MLA_ROPE // 2, 1) * sinhi)

    kpe_ref[...] = _bf16(rope64(zl[:, Q_LORA + KV_LORA:]))
    kn_ref[...] = _bf16(kv[:, :MLA_HEADS * MLA_NOPE])
    v_ref[...] = _bf16(kv[:, MLA_HEADS * MLA_NOPE:])
    for hd in range(MLA_HEADS):
        b0 = hd * MLA_QK_PAD
        q_ref[:, b0: b0 + LANES] = _bf16(q[:, b0: b0 + LANES])
        q_ref[:, b0 + LANES: b0 + 2 * LANES] = _bf16(rope64(q[:, b0 + LANES: b0 + 2 * LANES]))


def _in_proj(x2d, seq, ng, w_lat, w_ret, qg, kvg, wuq, wukv, tabs):
    tokens = x2d.shape[0]
    tm = ROW_TILE
    pos_tiles = seq // tm
    row = lambda width: pl.BlockSpec((tm, width), lambda i: (i, 0))
    pos = pl.BlockSpec((tm, LANES), lambda i: (i % pos_tiles, 0))
    bf = lambda width: jax.ShapeDtypeStruct((tokens, width), jnp.bfloat16)
    return pl.pallas_call(
        _in_proj_kernel,
        grid=(tokens // tm,),
        in_specs=[row(D_MODEL), _resident((1, D_MODEL)), _resident(w_lat.shape), _resident(w_ret.shape),
                  _resident((1, Q_LORA)), _resident((1, KV_LORA)),
                  _resident(wuq.shape), _resident(wukv.shape), pos, pos, pos, pos, pos],
        out_specs=[row(RET_WIDTH), row(RET_WIDTH), row(RET_WIDTH), row(MLA_HEADS * MLA_QK_PAD),
                   row(MLA_HEADS * MLA_NOPE), row(MLA_WIDTH), row(LANES)],
        out_shape=[bf(RET_WIDTH), bf(RET_WIDTH), bf(RET_WIDTH), bf(MLA_HEADS * MLA_QK_PAD),
                   bf(MLA_HEADS * MLA_NOPE), bf(MLA_WIDTH), bf(LANES)],
        compiler_params=pltpu.CompilerParams(dimension_semantics=("parallel",),
                                             vmem_limit_bytes=VMEM_LIMIT),
        name="in_proj",
    )(x2d, ng, w_lat, w_ret, qg, kvg, wuq, wukv, *tabs)


def _retention_kernel(q_ref, k_ref, v_ref, dmask_ref, qdec_ref, kdec_ref, sdec_ref, gn_ref,
                      o_ref, local_ref, state_ref, score_ref):
    c = RET_CHUNK
    n = q_ref.shape[0] // c
    rows = lambda i: slice(i * c, (i + 1) * c)
    kdec = kdec_ref[...]

    for i in range(n):
        kf = k_ref[rows(i), :].astype(jnp.float32)
        kz = jnp.concatenate([_bf16(kf * kdec[:, :RET_DK]), _bf16(kf * kdec[:, RET_DK:])], axis=1)
        local_ref[i] = lax.dot_general(kz, v_ref[rows(i), :], _TN,
                                       preferred_element_type=jnp.float32)

    dmask = dmask_ref[...]
    for i in range(n):
        s = lax.dot_general(q_ref[rows(i), :], k_ref[rows(i), :], _NT, preferred_element_type=jnp.float32)
        score_ref[i] = _bf16(s * dmask)

    sdec = sdec_ref[...]
    st = jnp.zeros((RET_DK, RET_DV), jnp.float32)
    for i in range(n):
        state_ref[i, :RET_DK, :] = _bf16(st)
        st = st * sdec[:RET_DK] + local_ref[i, :RET_DK, :]
    st = jnp.zeros((RET_DK, RET_DV), jnp.float32)
    for i in reversed(range(n)):
        state_ref[i, RET_DK:, :] = _bf16(st)
        st = st * sdec[RET_DK:] + local_ref[i, RET_DK:, :]

    qdec = qdec_ref[...]
    gain = gn_ref[...]
    for i in range(n):
        qf = q_ref[rows(i), :].astype(jnp.float32)
        qx = jnp.concatenate([_bf16(qf * qdec[:, :RET_DK]), _bf16(qf * qdec[:, RET_DK:])], axis=1)
        o = _dot(score_ref[i], v_ref[rows(i), :]) + _dot(qx, state_ref[i])
        mu = jnp.mean(o, axis=-1, keepdims=True)
        d = o - mu
        var = jnp.mean(d * d, axis=-1, keepdims=True)
        o_ref[rows(i), :] = _bf16(d * lax.rsqrt(var + NORM_EPS) * gain)


def _retention(rq, rk, rv, gn, dec):
    b, seq, _ = rq.shape
    c = RET_CHUNK
    n = seq // c
    head = pl.BlockSpec((None, seq, RET_DK), lambda i, h: (i, 0, h))
    per_head = lambda r, w: pl.BlockSpec((None, r, w), lambda i, h: (h, 0, 0))
    return pl.pallas_call(
        _retention_kernel,
        grid=(b, RET_HEADS),
        in_specs=[head, head, head, per_head(c, c), per_head(c, 2 * RET_DK), per_head(c, 2 * RET_DK),
                  per_head(2 * RET_DK, RET_DV), pl.BlockSpec((1, RET_DV), lambda i, h: (0, h))],
        out_specs=head,
        out_shape=jax.ShapeDtypeStruct((b, seq, RET_WIDTH), jnp.bfloat16),
        scratch_shapes=[pltpu.VMEM((n, 2 * RET_DK, RET_DV), jnp.float32),
                        pltpu.VMEM((n, 2 * RET_DK, RET_DV), jnp.bfloat16),
                        pltpu.VMEM((n, c, c), jnp.bfloat16)],
        compiler_params=pltpu.CompilerParams(dimension_semantics=("parallel", "arbitrary"),
                                             vmem_limit_bytes=VMEM_LIMIT),
        name="retention",
    )(rq, rk, rv, *dec, gn)


def _retention_tables():
    c = RET_CHUNK
    lg_f = jnp.log1p(-jnp.exp2(jnp.array(RET_LOG2_FWD, jnp.float32)))[:, None, None]
    lg_b = jnp.log1p(-jnp.exp2(jnp.array(RET_LOG2_BWD, jnp.float32)))[:, None, None]
    pos = jnp.arange(c, dtype=jnp.float32)
    diff = pos[:, None] - pos[None, :]
    dmask = jnp.where(diff >= 0, jnp.exp(lg_f * jnp.maximum(diff, 0.0)),
                      jnp.exp(lg_b * jnp.maximum(-diff, 0.0)))
    col = pos[None, :, None]
    wide = lambda f, b_: jnp.concatenate([jnp.broadcast_to(f, (RET_HEADS, c, RET_DK)),
                                          jnp.broadcast_to(b_, (RET_HEADS, c, RET_DK))], axis=2)
    qdec = wide(jnp.exp(lg_f * (col + 1.0)), jnp.exp(lg_b * (c - col)))
    kdec = wide(jnp.exp(lg_f * (c - 1.0 - col)), jnp.exp(lg_b * col))
    sdec = jnp.concatenate([jnp.broadcast_to(jnp.exp(lg_f * c), (RET_HEADS, RET_DK, RET_DV)),
                            jnp.broadcast_to(jnp.exp(lg_b * c), (RET_HEADS, RET_DK, RET_DV))], axis=1)
    return dmask, qdec, kdec, sdec


def _attention_kernel(q_ref, kn_ref, kpe_ref, v_ref, o_ref, kcat_ref, vt_ref):
    seq = q_ref.shape[0]
    tq, tk = min(ATT_Q_TILE, seq), ATT_K_TILE
    kcat_ref[:, :LANES] = kn_ref[...]
    kcat_ref[:, LANES:] = kpe_ref[...]
    for j in range(seq // tk):
        vt_ref[j, :MLA_V, :] = _bf16(v_ref[j * tk:(j + 1) * tk, :].astype(jnp.float32).T)
        vt_ref[j, MLA_V:, :] = jnp.ones((ATT_ONES_ROWS, tk), jnp.bfloat16)

    def q_tile(i, carry):
        qrows = pl.ds(pl.multiple_of(i * tq, tq), tq)
        qt = _bf16(q_ref[qrows, :].astype(jnp.float32).T)

        def scores(j):
            return _dot(kcat_ref[j * tk:(j + 1) * tk, :], qt)

        m = jnp.full((1, tq), -jnp.inf, jnp.float32)
        acc = jnp.zeros((MLA_V + ATT_ONES_ROWS, tq), jnp.float32)
        n_k = seq // tk
        s = scores(0)
        for j in range(n_k):
            s_next = scores(j + 1) if j + 1 < n_k else None
            m_new = jnp.maximum(m, jnp.max(s, axis=0, keepdims=True))
            alpha = jnp.exp2(m - m_new)
            p = _bf16(jnp.exp2(s - m_new))
            acc = alpha * acc + _dot(vt_ref[j], p)
            m, s = m_new, s_next
        o_ref[qrows, :] = _bf16((acc[:MLA_V] / acc[MLA_V:MLA_V + 1]).T)
        return carry

    lax.fori_loop(0, seq // tq, q_tile, 0)


def _attention(q, kn, kpe, v):
    b, seq, _ = q.shape
    tk = ATT_K_TILE
    head = lambda w: pl.BlockSpec((None, seq, w), lambda i, h: (i, 0, h))
    return pl.pallas_call(
        _attention_kernel,
        grid=(b, MLA_HEADS),
        in_specs=[head(MLA_QK_PAD), head(MLA_NOPE),
                  pl.BlockSpec((None, seq, LANES), lambda i, h: (i, 0, 0)), head(MLA_V)],
        out_specs=head(MLA_V),
        out_shape=jax.ShapeDtypeStruct((b, seq, MLA_WIDTH), jnp.bfloat16),
        scratch_shapes=[pltpu.VMEM((seq, MLA_QK_PAD), jnp.bfloat16),
                        pltpu.VMEM((seq // tk, MLA_V + ATT_ONES_ROWS, tk), jnp.bfloat16)],
        compiler_params=pltpu.CompilerParams(dimension_semantics=("parallel", "arbitrary"),
                                             vmem_limit_bytes=VMEM_LIMIT),
        name="attention",
    )(q, kn, kpe, v)


def _out_proj_kernel(x_ref, ret_ref, att_ref, ng_ref, wg_ref, wbr_ref, wbm_ref, wo_ref, fg_ref,
                     y_ref, *, final_norm):
    x = x_ref[...]
    h = _bf16(_rms(x, ng_ref[...]))
    g = _dot(h, wg_ref[...])
    rg = g[:, :RET_WIDTH]
    mg = g[:, RET_WIDTH: RET_WIDTH + MLA_WIDTH]
    g_ret = g[:, RET_WIDTH + MLA_WIDTH: RET_WIDTH + MLA_WIDTH + D_MODEL]
    g_mla = g[:, RET_WIDTH + MLA_WIDTH + D_MODEL:]
    ret_out = _dot(_bf16(ret_ref[...].astype(jnp.float32) * (rg * jax.nn.sigmoid(rg))), wbr_ref[...])
    mla_out = _dot(_bf16(att_ref[...].astype(jnp.float32) * (mg * jax.nn.sigmoid(mg))), wbm_ref[...])
    merged = jax.nn.sigmoid(g_ret) * ret_out + jax.nn.sigmoid(g_mla) * mla_out
    y = x + _dot(_bf16(merged), wo_ref[...])
    if final_norm:
        y = _rms(y, fg_ref[...])
    y_ref[...] = y


def _out_proj(x2d, ret, att, ng, w_gate, w_br_ret, w_br_mla, w_out, fg, final_norm):
    tokens = x2d.shape[0]
    tm = ROW_TILE
    row = lambda width: pl.BlockSpec((tm, width), lambda i: (i, 0))
    return pl.pallas_call(
        functools.partial(_out_proj_kernel, final_norm=final_norm),
        grid=(tokens // tm,),
        in_specs=[row(D_MODEL), row(RET_WIDTH), row(MLA_WIDTH), _resident((1, D_MODEL)),
                  _resident(w_gate.shape), _resident(w_br_ret.shape), _resident(w_br_mla.shape),
                  _resident(w_out.shape), _resident((1, D_MODEL))],
        out_specs=row(D_MODEL),
        out_shape=jax.ShapeDtypeStruct((tokens, D_MODEL), jnp.float32),
        compiler_params=pltpu.CompilerParams(dimension_semantics=("parallel",),
                                             vmem_limit_bytes=VMEM_LIMIT),
        name="out_proj",
    )(x2d, ret, att, ng, w_gate, w_br_ret, w_br_mla, w_out, fg)


def _rope_tables(seq):
    posn = jnp.arange(seq, dtype=jnp.float32)[:, None]

    def angles(d):
        inv = ROPE_BASE ** (-jnp.arange(0, d, 2, dtype=jnp.float32) / d)
        return posn * inv[None, :]

    ar = angles(RET_DK)
    cos_r = jnp.concatenate([jnp.cos(ar), jnp.cos(ar)], axis=1)
    sin_r = jnp.concatenate([-jnp.sin(ar), jnp.sin(ar)], axis=1)
    ap = angles(MLA_ROPE)
    z32 = jnp.zeros_like(ap)
    z64 = jnp.zeros((seq, LANES - MLA_ROPE), jnp.float32)
    cos_p = jnp.concatenate([jnp.cos(ap), jnp.cos(ap), z64], axis=1)
    sin_lo = jnp.concatenate([-jnp.sin(ap), z32, z64], axis=1)
    sin_hi = jnp.concatenate([z32, jnp.sin(ap), z64], axis=1)
    return cos_r, sin_r, cos_p, sin_lo, sin_hi


def _prep_layer(w_in, w_uq, w_ukv):
    zpad = jnp.zeros((D_MODEL, LANES - MLA_ROPE), w_in.dtype)
    w_lat = jnp.concatenate([w_in[:, O_CQ:O_MG], zpad], axis=1)
    w_ret = w_in[:, O_RQ:O_RG]
    w_gate = jnp.concatenate([w_in[:, O_RG:O_CQ], w_in[:, O_MG:]], axis=1)
    uq = w_uq.reshape(Q_LORA, MLA_HEADS, MLA_QK)
    uq = jnp.concatenate([uq, jnp.zeros((Q_LORA, MLA_HEADS, MLA_QK_PAD - MLA_QK), w_uq.dtype)], axis=2)
    ukv = w_ukv.reshape(KV_LORA, MLA_HEADS, MLA_NOPE + MLA_V)
    ukv = jnp.concatenate([ukv[:, :, :MLA_NOPE].reshape(KV_LORA, -1),
                           ukv[:, :, MLA_NOPE:].reshape(KV_LORA, -1)], axis=1)
    return _bf16(w_lat), _bf16(w_ret), _bf16(w_gate), _bf16(uq.reshape(Q_LORA, -1)), _bf16(ukv)


def _trunk(x, layers, final_norm_g, ret_tabs):
    b, seq, _ = x.shape
    rope_tabs = _rope_tables(seq)
    x2d = x.reshape(b * seq, D_MODEL)
    for i, lw in enumerate(layers):
        rq, rk, rv, q, kn, v, kpe = _in_proj(x2d, seq, lw["ng"], lw["w_lat"], lw["w_ret"], lw["qg"],
                                             lw["kvg"], lw["wuq"], lw["wukv"], rope_tabs)
        sh = lambda t: t.reshape(b, seq, t.shape[-1])
        ret = _retention(sh(rq), sh(rk), sh(rv), lw["gn"], ret_tabs)
        att = _attention(sh(q), sh(kn), sh(kpe), sh(v))
        x2d = _out_proj(x2d, ret.reshape(b * seq, -1), att.reshape(b * seq, -1), lw["ng"], lw["w_gate"],
                        lw["wbr"], lw["wbm"], lw["wo"], final_norm_g, final_norm=(i == DEPTH - 1))
    return x2d.reshape(b, seq, D_MODEL)


def kernel(x_prompt, x_sample, norm_g, w_in, ret_gn_g, q_norm_g, kv_norm_g, w_uq, w_ukv, w_br_ret, w_br_mla,
           w_out, final_norm_g):
    layers = []
    for i in range(DEPTH):
        w_lat, w_ret, w_gate, wuq, wukv = _prep_layer(w_in[i], w_uq[i], w_ukv[i])
        qg = q_norm_g[i][None, :] * (MLA_QK ** -0.5 * math.log2(math.e))
        layers.append(dict(ng=norm_g[i][None, :], qg=qg, kvg=kv_norm_g[i][None, :],
                           gn=ret_gn_g[i][None, :], w_lat=w_lat, w_ret=w_ret, w_gate=w_gate, wuq=wuq, wukv=wukv,
                           wbr=_bf16(w_br_ret[i]), wbm=_bf16(w_br_mla[i]), wo=_bf16(w_out[i])))
    fg = final_norm_g[None, :]
    ret_tabs = _retention_tables()
    return (_trunk(x_prompt, layers, fg, ret_tabs), _trunk(x_sample, layers, fg, ret_tabs))
```

```python
import functools
import math

import jax
import jax.numpy as jnp
import numpy as np
from jax import lax
from jax.experimental import pallas as pl
from jax.experimental.pallas import tpu as pltpu

D_MODEL = 1024
DEPTH = 2
RET_HEADS = 4
RET_DK = 128
RET_DV = 128
RET_WIDTH = RET_HEADS * RET_DV
RET_LOG2_FWD = (-5.0, -6.0, -7.0, -8.0)
RET_LOG2_BWD = (-5.5, -6.5, -7.5, -8.5)
MLA_HEADS = 4
MLA_NOPE = 128
MLA_ROPE = 64
MLA_QK = MLA_NOPE + MLA_ROPE
MLA_V = 128
MLA_WIDTH = MLA_HEADS * MLA_V
Q_LORA = 256
KV_LORA = 128
ROPE_BASE = 10000.0
NORM_EPS = 1e-6

LANES = 128
MLA_QK_PAD = 2 * LANES
_OFF = np.cumsum((0, 512, 512, 512, 512, Q_LORA, KV_LORA, MLA_ROPE, MLA_WIDTH, D_MODEL, D_MODEL))
(O_RQ, O_RK, O_RV, O_RG, O_CQ, O_CKV, O_KPE, O_MG, O_GRET, O_GMLA, _) = [int(v) for v in _OFF]

ROW_TILE = 1024
RET_CHUNK = 256
ATT_Q_TILE = 2048
ATT_K_TILE = 512
ATT_Q_GROUP = 256
ATT_ONES_ROWS = 16
VMEM_LIMIT = 56 * 1024 * 1024

_NT = (((1,), (1,)), ((), ()))
_TN = (((0,), (0,)), ((), ()))


def _bf16(x):
    return x.astype(jnp.bfloat16)


def _dot(a, b):
    return jnp.dot(a, b, preferred_element_type=jnp.float32)


def _rms(x, g):
    return x * lax.rsqrt(jnp.mean(x * x, axis=-1, keepdims=True) + NORM_EPS) * g


def _resident(shape):
    return pl.BlockSpec(shape, lambda *_: (0,) * len(shape))


def _in_proj_kernel(x_ref, ng_ref, wlat_ref, wret_ref, qg_ref, kvg_ref, wuq_ref, wukn_ref, wuvt_ref,
                    cosr_ref, sinr_ref, cosp_ref, sinlo_ref, sinhi_ref,
                    rq_ref, rk_ref, rv_ref, q_ref, k_ref, vt_ref):
    h = _bf16(_rms(x_ref[...], ng_ref[...]))
    zl = _dot(h, wlat_ref[...])
    zr = _dot(h, wret_ref[...])

    cq = _bf16(_rms(zl[:, :Q_LORA], qg_ref[...]))
    ckv = _bf16(_rms(zl[:, Q_LORA: Q_LORA + KV_LORA], kvg_ref[...]))
    q = _dot(cq, wuq_ref[...])
    kn = _dot(ckv, wukn_ref[...])
    vt_ref[...] = _bf16(lax.dot_general(wuvt_ref[...], ckv, _NT, preferred_element_type=jnp.float32))

    cosr, sinr = cosr_ref[...], sinr_ref[...]
    k_scale = RET_DK ** -0.5
    for hd in range(RET_HEADS):
        sl = slice(hd * RET_DK, (hd + 1) * RET_DK)
        xq = zr[:, sl]
        rq_ref[:, sl] = _bf16(xq * cosr + pltpu.roll(xq, RET_DK // 2, 1) * sinr)
        xk = zr[:, RET_WIDTH + hd * RET_DK: RET_WIDTH + (hd + 1) * RET_DK]
        rk_ref[:, sl] = _bf16((xk * cosr + pltpu.roll(xk, RET_DK // 2, 1) * sinr) * k_scale)
    rv_ref[...] = _bf16(zr[:, 2 * RET_WIDTH:])

    cosp, sinlo, sinhi = cosp_ref[...], sinlo_ref[...], sinhi_ref[...]

    def rope64(pe):
        return (pe * cosp + pltpu.roll(pe, LANES - MLA_ROPE // 2, 1) * sinlo
                + pltpu.roll(pe, MLA_ROPE // 2, 1) * sinhi)

    kpe = _bf16(rope64(zl[:, Q_LORA + KV_LORA:]))
    for hd in range(MLA_HEADS):
        b0 = hd * MLA_QK_PAD
        k_ref[:, b0: b0 + LANES] = _bf16(kn[:, hd * MLA_NOPE: (hd + 1) * MLA_NOPE])
        k_ref[:, b0 + LANES: b0 + 2 * LANES] = kpe
        q_ref[:, b0: b0 + LANES] = _bf16(q[:, b0: b0 + LANES])
        q_ref[:, b0 + LANES: b0 + 2 * LANES] = _bf16(rope64(q[:, b0 + LANES: b0 + 2 * LANES]))


def _in_proj(x2d, seq, ng, w_lat, w_ret, qg, kvg, wuq, wukn, wuvt, tabs):
    tokens = x2d.shape[0]
    tm = ROW_TILE
    pos_tiles = seq // tm
    row = lambda width: pl.BlockSpec((tm, width), lambda i: (i, 0))
    pos = pl.BlockSpec((tm, LANES), lambda i: (i % pos_tiles, 0))
    bf = lambda width: jax.ShapeDtypeStruct((tokens, width), jnp.bfloat16)
    return pl.pallas_call(
        _in_proj_kernel,
        grid=(tokens // tm,),
        in_specs=[row(D_MODEL), _resident((1, D_MODEL)), _resident(w_lat.shape), _resident(w_ret.shape),
                  _resident((1, Q_LORA)), _resident((1, KV_LORA)),
                  _resident(wuq.shape), _resident(wukn.shape), _resident(wuvt.shape), pos, pos, pos, pos, pos],
        out_specs=[row(RET_WIDTH), row(RET_WIDTH), row(RET_WIDTH), row(MLA_HEADS * MLA_QK_PAD),
                   row(MLA_HEADS * MLA_QK_PAD), pl.BlockSpec((MLA_WIDTH, tm), lambda i: (0, i))],
        out_shape=[bf(RET_WIDTH), bf(RET_WIDTH), bf(RET_WIDTH), bf(MLA_HEADS * MLA_QK_PAD),
                   bf(MLA_HEADS * MLA_QK_PAD), jax.ShapeDtypeStruct((MLA_WIDTH, tokens), jnp.bfloat16)],
        compiler_params=pltpu.CompilerParams(dimension_semantics=("parallel",),
                                             vmem_limit_bytes=VMEM_LIMIT),
        name="in_proj",
    )(x2d, ng, w_lat, w_ret, qg, kvg, wuq, wukn, wuvt, *tabs)


def _retention_kernel(q_ref, k_ref, v_ref, dmask_ref, qdec_ref, kdec_ref, sdec_ref, gn_ref,
                      o_ref, local_ref, state_ref, score_ref):
    c = RET_CHUNK
    n = q_ref.shape[0] // c
    rows = lambda i: slice(i * c, (i + 1) * c)
    kdec = kdec_ref[...]

    for i in range(n):
        kf = k_ref[rows(i), :].astype(jnp.float32)
        kz = jnp.concatenate([_bf16(kf * kdec[:, :RET_DK]), _bf16(kf * kdec[:, RET_DK:])], axis=1)
        local_ref[i] = lax.dot_general(kz, v_ref[rows(i), :], _TN,
                                       preferred_element_type=jnp.float32)

    dmask = dmask_ref[...]
    for i in range(n):
        s = lax.dot_general(q_ref[rows(i), :], k_ref[rows(i), :], _NT, preferred_element_type=jnp.float32)
        score_ref[i] = _bf16(s * dmask)

    sdec = sdec_ref[...]
    st = jnp.zeros((RET_DK, RET_DV), jnp.float32)
    for i in range(n):
        state_ref[i, :RET_DK, :] = _bf16(st)
        st = st * sdec[:RET_DK] + local_ref[i, :RET_DK, :]
    st = jnp.zeros((RET_DK, RET_DV), jnp.float32)
    for i in reversed(range(n)):
        state_ref[i, RET_DK:, :] = _bf16(st)
        st = st * sdec[RET_DK:] + local_ref[i, RET_DK:, :]

    qdec = qdec_ref[...]
    gain = gn_ref[...]
    for i in range(n):
        qf = q_ref[rows(i), :].astype(jnp.float32)
        qx = jnp.concatenate([_bf16(qf * qdec[:, :RET_DK]), _bf16(qf * qdec[:, RET_DK:])], axis=1)
        o = _dot(score_ref[i], v_ref[rows(i), :]) + _dot(qx, state_ref[i])
        mu = jnp.mean(o, axis=-1, keepdims=True)
        d = o - mu
        var = jnp.mean(d * d, axis=-1, keepdims=True)
        o_ref[rows(i), :] = _bf16(d * lax.rsqrt(var + NORM_EPS) * gain)


def _retention(rq, rk, rv, gn, dec):
    b, seq, _ = rq.shape
    c = RET_CHUNK
    n = seq // c
    head = pl.BlockSpec((None, seq, RET_DK), lambda i, h: (i, 0, h))
    per_head = lambda r, w: pl.BlockSpec((None, r, w), lambda i, h: (h, 0, 0))
    return pl.pallas_call(
        _retention_kernel,
        grid=(b, RET_HEADS),
        in_specs=[head, head, head, per_head(c, c), per_head(c, 2 * RET_DK), per_head(c, 2 * RET_DK),
                  per_head(2 * RET_DK, RET_DV), pl.BlockSpec((1, RET_DV), lambda i, h: (0, h))],
        out_specs=head,
        out_shape=jax.ShapeDtypeStruct((b, seq, RET_WIDTH), jnp.bfloat16),
        scratch_shapes=[pltpu.VMEM((n, 2 * RET_DK, RET_DV), jnp.float32),
                        pltpu.VMEM((n, 2 * RET_DK, RET_DV), jnp.bfloat16),
                        pltpu.VMEM((n, c, c), jnp.bfloat16)],
        compiler_params=pltpu.CompilerParams(dimension_semantics=("parallel", "arbitrary"),
                                             vmem_limit_bytes=VMEM_LIMIT),
        name="retention",
    )(rq, rk, rv, *dec, gn)


def _retention_tables():
    c = RET_CHUNK
    lg_f = jnp.log1p(-jnp.exp2(jnp.array(RET_LOG2_FWD, jnp.float32)))[:, None, None]
    lg_b = jnp.log1p(-jnp.exp2(jnp.array(RET_LOG2_BWD, jnp.float32)))[:, None, None]
    pos = jnp.arange(c, dtype=jnp.float32)
    diff = pos[:, None] - pos[None, :]
    dmask = jnp.where(diff >= 0, jnp.exp(lg_f * jnp.maximum(diff, 0.0)),
                      jnp.exp(lg_b * jnp.maximum(-diff, 0.0)))
    col = pos[None, :, None]
    wide = lambda f, b_: jnp.concatenate([jnp.broadcast_to(f, (RET_HEADS, c, RET_DK)),
                                          jnp.broadcast_to(b_, (RET_HEADS, c, RET_DK))], axis=2)
    qdec = wide(jnp.exp(lg_f * (col + 1.0)), jnp.exp(lg_b * (c - col)))
    kdec = wide(jnp.exp(lg_f * (c - 1.0 - col)), jnp.exp(lg_b * col))
    sdec = jnp.concatenate([jnp.broadcast_to(jnp.exp(lg_f * c), (RET_HEADS, RET_DK, RET_DV)),
                            jnp.broadcast_to(jnp.exp(lg_b * c), (RET_HEADS, RET_DK, RET_DV))], axis=1)
    return dmask, qdec, kdec, sdec


def _attention_kernel(q_ref, k_ref, vt_ref, o_ref):
    seq = q_ref.shape[0]
    tq, tk = min(ATT_Q_TILE, seq), ATT_K_TILE
    n_k = seq // tk
    ones = jnp.ones((ATT_ONES_ROWS, tk), jnp.bfloat16)

    def q_tile(i, carry):
        qrows = pl.ds(pl.multiple_of(i * tq, tq), tq)
        qt = _bf16(q_ref[qrows, :].astype(jnp.float32).T)

        ws = ATT_Q_GROUP
        ns = tq // ws
        qts = [qt[:, u * ws:(u + 1) * ws] for u in range(ns)]

        def scores(j, u):
            return _dot(k_ref[j * tk:(j + 1) * tk, :], qts[u])

        m = [jnp.full((1, ws), -jnp.inf, jnp.float32)] * ns
        acc = [jnp.zeros((MLA_V + ATT_ONES_ROWS, ws), jnp.float32)] * ns
        s = [scores(0, u) for u in range(ns)]
        for j in range(n_k):
            vt1 = jnp.concatenate([vt_ref[:, j * tk:(j + 1) * tk], ones], axis=0)
            for u in range(ns):
                s_next = scores(j + 1, u) if j + 1 < n_k else None
                m_new = jnp.maximum(m[u], jnp.max(s[u], axis=0, keepdims=True))
                alpha = jnp.exp2(m[u] - m_new)
                p = _bf16(jnp.exp2(s[u] - m_new))
                acc[u] = alpha * acc[u] + _dot(vt1, p)
                m[u], s[u] = m_new, s_next
        out = jnp.concatenate([a[:MLA_V] / a[MLA_V:MLA_V + 1] for a in acc], axis=1)
        o_ref[qrows, :] = _bf16(out.T)
        return carry

    lax.fori_loop(0, seq // tq, q_tile, 0)


def _attention(q, k, vt):
    b, seq, _ = q.shape
    head = lambda w: pl.BlockSpec((None, seq, w), lambda i, h: (i, 0, h))
    return pl.pallas_call(
        _attention_kernel,
        grid=(b, MLA_HEADS),
        in_specs=[head(MLA_QK_PAD), head(MLA_QK_PAD), pl.BlockSpec((MLA_V, seq), lambda i, h: (h, i))],
        out_specs=head(MLA_V),
        out_shape=jax.ShapeDtypeStruct((b, seq, MLA_WIDTH), jnp.bfloat16),
        compiler_params=pltpu.CompilerParams(dimension_semantics=("parallel", "arbitrary"),
                                             vmem_limit_bytes=VMEM_LIMIT),
        name="attention",
    )(q, k, vt)


def _out_proj_kernel(x_ref, ret_ref, att_ref, ng_ref, wg_ref, wbr_ref, wbm_ref, wo_ref, fg_ref,
                     y_ref, *, final_norm):
    x = x_ref[...]
    h = _bf16(_rms(x, ng_ref[...]))
    g = _dot(h, wg_ref[...])
    rg = g[:, :RET_WIDTH]
    mg = g[:, RET_WIDTH: RET_WIDTH + MLA_WIDTH]
    g_ret = g[:, RET_WIDTH + MLA_WIDTH: RET_WIDTH + MLA_WIDTH + D_MODEL]
    g_mla = g[:, RET_WIDTH + MLA_WIDTH + D_MODEL:]
    ret_out = _dot(_bf16(ret_ref[...].astype(jnp.float32) * (rg * jax.nn.sigmoid(rg))), wbr_ref[...])
    mla_out = _dot(_bf16(att_ref[...].astype(jnp.float32) * (mg * jax.nn.sigmoid(mg))), wbm_ref[...])
    merged = jax.nn.sigmoid(g_ret) * ret_out + jax.nn.sigmoid(g_mla) * mla_out
    y = x + _dot(_bf16(merged), wo_ref[...])
    if final_norm:
        y = _rms(y, fg_ref[...])
    y_ref[...] = y


def _out_proj(x2d, ret, att, ng, w_gate, w_br_ret, w_br_mla, w_out, fg, final_norm):
    tokens = x2d.shape[0]
    tm = ROW_TILE
    row = lambda width: pl.BlockSpec((tm, width), lambda i: (i, 0))
    return pl.pallas_call(
        functools.partial(_out_proj_kernel, final_norm=final_norm),
        grid=(tokens // tm,),
        in_specs=[row(D_MODEL), row(RET_WIDTH), row(MLA_WIDTH), _resident((1, D_MODEL)),
                  _resident(w_gate.shape), _resident(w_br_ret.shape), _resident(w_br_mla.shape),
                  _resident(w_out.shape), _resident((1, D_MODEL))],
        out_specs=row(D_MODEL),
        out_shape=jax.ShapeDtypeStruct((tokens, D_MODEL), jnp.float32),
        compiler_params=pltpu.CompilerParams(dimension_semantics=("parallel",),
                                             vmem_limit_bytes=VMEM_LIMIT),
        name="out_proj",
    )(x2d, ret, att, ng, w_gate, w_br_ret, w_br_mla, w_out, fg)


def _rope_tables(seq):
    posn = jnp.arange(seq, dtype=jnp.float32)[:, None]

    def angles(d):
        inv = ROPE_BASE ** (-jnp.arange(0, d, 2, dtype=jnp.float32) / d)
        return posn * inv[None, :]

    ar = angles(RET_DK)
    cos_r = jnp.concatenate([jnp.cos(ar), jnp.cos(ar)], axis=1)
    sin_r = jnp.concatenate([-jnp.sin(ar), jnp.sin(ar)], axis=1)
    ap = angles(MLA_ROPE)
    z32 = jnp.zeros_like(ap)
    z64 = jnp.zeros((seq, LANES - MLA_ROPE), jnp.float32)
    cos_p = jnp.concatenate([jnp.cos(ap), jnp.cos(ap), z64], axis=1)
    sin_lo = jnp.concatenate([-jnp.sin(ap), z32, z64], axis=1)
    sin_hi = jnp.concatenate([z32, jnp.sin(ap), z64], axis=1)
    return cos_r, sin_r, cos_p, sin_lo, sin_hi


def _prep_layer(w_in, w_uq, w_ukv):
    zpad = jnp.zeros((D_MODEL, LANES - MLA_ROPE), w_in.dtype)
    w_lat = jnp.concatenate([w_in[:, O_CQ:O_MG], zpad], axis=1)
    w_ret = w_in[:, O_RQ:O_RG]
    w_gate = jnp.concatenate([w_in[:, O_RG:O_CQ], w_in[:, O_MG:]], axis=1)
    uq = w_uq.reshape(Q_LORA, MLA_HEADS, MLA_QK)
    uq = jnp.concatenate([uq, jnp.zeros((Q_LORA, MLA_HEADS, MLA_QK_PAD - MLA_QK), w_uq.dtype)], axis=2)
    ukv = w_ukv.reshape(KV_LORA, MLA_HEADS, MLA_NOPE + MLA_V)
    ukn = ukv[:, :, :MLA_NOPE].reshape(KV_LORA, -1)
    uvt = ukv[:, :, MLA_NOPE:].reshape(KV_LORA, -1).T
    return (_bf16(w_lat), _bf16(w_ret), _bf16(w_gate), _bf16(uq.reshape(Q_LORA, -1)), _bf16(ukn),
            _bf16(uvt))


def _trunk(x, layers, final_norm_g, ret_tabs):
    b, seq, _ = x.shape
    rope_tabs = _rope_tables(seq)
    x2d = x.reshape(b * seq, D_MODEL)
    for i, lw in enumerate(layers):
        rq, rk, rv, q, k, vt = _in_proj(x2d, seq, lw["ng"], lw["w_lat"], lw["w_ret"], lw["qg"],
                                        lw["kvg"], lw["wuq"], lw["wukn"], lw["wuvt"], rope_tabs)
        sh = lambda t: t.reshape(b, seq, t.shape[-1])
        ret = _retention(sh(rq), sh(rk), sh(rv), lw["gn"], ret_tabs)
        att = _attention(sh(q), sh(k), vt)
        x2d = _out_proj(x2d, ret.reshape(b * seq, -1), att.reshape(b * seq, -1), lw["ng"], lw["w_gate"],
                        lw["wbr"], lw["wbm"], lw["wo"], final_norm_g, final_norm=(i == DEPTH - 1))
    return x2d.reshape(b, seq, D_MODEL)


def kernel(x_prompt, x_sample, norm_g, w_in, ret_gn_g, q_norm_g, kv_norm_g, w_uq, w_ukv, w_br_ret, w_br_mla,
           w_out, final_norm_g):
    layers = []
    for i in range(DEPTH):
        w_lat, w_ret, w_gate, wuq, wukn, wuvt = _prep_layer(w_in[i], w_uq[i], w_ukv[i])
        qg = q_norm_g[i][None, :] * (MLA_QK ** -0.5 * math.log2(math.e))
        layers.append(dict(ng=norm_g[i][None, :], qg=qg, kvg=kv_norm_g[i][None, :],
                           gn=ret_gn_g[i][None, :], w_lat=w_lat, w_ret=w_ret, w_gate=w_gate, wuq=wuq, wukn=wukn,
                           wuvt=wuvt,
                           wbr=_bf16(w_br_ret[i]), wbm=_bf16(w_br_mla[i]), wo=_bf16(w_out[i])))
    fg = final_norm_g[None, :]
    ret_tabs = _retention_tables()
    return (_trunk(x_prompt, layers, fg, ret_tabs), _trunk(x_sample, layers, fg, ret_tabs))
```

```python
import functools
import math

import jax
import jax.numpy as jnp
import numpy as np
from jax import lax
from jax.experimental import pallas as pl
from jax.experimental.pallas import tpu as pltpu

D_MODEL = 1024
DEPTH = 2
RET_HEADS = 4
RET_DK = 128
RET_DV = 128
RET_WIDTH = RET_HEADS * RET_DV
RET_LOG2_FWD = (-5.0, -6.0, -7.0, -8.0)
RET_LOG2_BWD = (-5.5, -6.5, -7.5, -8.5)
MLA_HEADS = 4
MLA_NOPE = 128
MLA_ROPE = 64
MLA_QK = MLA_NOPE + MLA_ROPE
MLA_V = 128
MLA_WIDTH = MLA_HEADS * MLA_V
Q_LORA = 256
KV_LORA = 128
ROPE_BASE = 10000.0
NORM_EPS = 1e-6

LANES = 128
MLA_QK_PAD = 2 * LANES
_OFF = np.cumsum((0, 512, 512, 512, 512, Q_LORA, KV_LORA, MLA_ROPE, MLA_WIDTH, D_MODEL, D_MODEL))
(O_RQ, O_RK, O_RV, O_RG, O_CQ, O_CKV, O_KPE, O_MG, O_GRET, O_GMLA, _) = [int(v) for v in _OFF]

ROW_TILE = 1024
RET_CHUNK = 256
RET_CHUNKS_PER_BLOCK = 16
ATT_Q_TILE = 2048
ATT_K_TILE = 512
ATT_Q_GROUP = 256
ATT_ONES_ROWS = 16
VMEM_LIMIT = 56 * 1024 * 1024

_NT = (((1,), (1,)), ((), ()))
_TN = (((0,), (0,)), ((), ()))


def _bf16(x):
    return x.astype(jnp.bfloat16)


def _dot(a, b):
    return jnp.dot(a, b, preferred_element_type=jnp.float32)


def _rms(x, g):
    return x * lax.rsqrt(jnp.mean(x * x, axis=-1, keepdims=True) + NORM_EPS) * g


def _resident(shape):
    return pl.BlockSpec(shape, lambda *_: (0,) * len(shape))


def _in_proj_kernel(x_ref, ng_ref, wlat_ref, wret_ref, qg_ref, kvg_ref, wuq_ref, wukn_ref, wuvt_ref,
                    cosr_ref, sinr_ref, cosp_ref, sinlo_ref, sinhi_ref,
                    rq_ref, rk_ref, rv_ref, q_ref, k_ref, vt_ref):
    h = _bf16(_rms(x_ref[...], ng_ref[...]))
    zl = _dot(h, wlat_ref[...])
    zr = _dot(h, wret_ref[...])

    cq = _bf16(_rms(zl[:, :Q_LORA], qg_ref[...]))
    ckv = _bf16(_rms(zl[:, Q_LORA: Q_LORA + KV_LORA], kvg_ref[...]))
    q = _dot(cq, wuq_ref[...])
    kn = _dot(ckv, wukn_ref[...])
    vt_ref[...] = _bf16(lax.dot_general(wuvt_ref[...], ckv, _NT, preferred_element_type=jnp.float32))

    cosr, sinr = cosr_ref[...], sinr_ref[...]
    k_scale = RET_DK ** -0.5
    for hd in range(RET_HEADS):
        sl = slice(hd * RET_DK, (hd + 1) * RET_DK)
        xq = zr[:, sl]
        rq_ref[:, sl] = _bf16(xq * cosr + pltpu.roll(xq, RET_DK // 2, 1) * sinr)
        xk = zr[:, RET_WIDTH + hd * RET_DK: RET_WIDTH + (hd + 1) * RET_DK]
        rk_ref[:, sl] = _bf16((xk * cosr + pltpu.roll(xk, RET_DK // 2, 1) * sinr) * k_scale)
    rv_ref[...] = _bf16(zr[:, 2 * RET_WIDTH:])

    cosp, sinlo, sinhi = cosp_ref[...], sinlo_ref[...], sinhi_ref[...]

    def rope64(pe):
        return (pe * cosp + pltpu.roll(pe, LANES - MLA_ROPE // 2, 1) * sinlo
                + pltpu.roll(pe, MLA_ROPE // 2, 1) * sinhi)

    kpe = _bf16(rope64(zl[:, Q_LORA + KV_LORA:]))
    for hd in range(MLA_HEADS):
        b0 = hd * MLA_QK_PAD
        k_ref[:, b0: b0 + LANES] = _bf16(kn[:, hd * MLA_NOPE: (hd + 1) * MLA_NOPE])
        k_ref[:, b0 + LANES: b0 + 2 * LANES] = kpe
        q_ref[:, b0: b0 + LANES] = _bf16(q[:, b0: b0 + LANES])
        q_ref[:, b0 + LANES: b0 + 2 * LANES] = _bf16(rope64(q[:, b0 + LANES: b0 + 2 * LANES]))


def _in_proj(x2d, seq, ng, w_lat, w_ret, qg, kvg, wuq, wukn, wuvt, tabs):
    tokens = x2d.shape[0]
    tm = ROW_TILE
    pos_tiles = seq // tm
    row = lambda width: pl.BlockSpec((tm, width), lambda i: (i, 0))
    pos = pl.BlockSpec((tm, LANES), lambda i: (i % pos_tiles, 0))
    bf = lambda width: jax.ShapeDtypeStruct((tokens, width), jnp.bfloat16)
    return pl.pallas_call(
        _in_proj_kernel,
        grid=(tokens // tm,),
        in_specs=[row(D_MODEL), _resident((1, D_MODEL)), _resident(w_lat.shape), _resident(w_ret.shape),
                  _resident((1, Q_LORA)), _resident((1, KV_LORA)),
                  _resident(wuq.shape), _resident(wukn.shape), _resident(wuvt.shape), pos, pos, pos, pos, pos],
        out_specs=[row(RET_WIDTH), row(RET_WIDTH), row(RET_WIDTH), row(MLA_HEADS * MLA_QK_PAD),
                   row(MLA_HEADS * MLA_QK_PAD), pl.BlockSpec((MLA_WIDTH, tm), lambda i: (0, i))],
        out_shape=[bf(RET_WIDTH), bf(RET_WIDTH), bf(RET_WIDTH), bf(MLA_HEADS * MLA_QK_PAD),
                   bf(MLA_HEADS * MLA_QK_PAD), jax.ShapeDtypeStruct((MLA_WIDTH, tokens), jnp.bfloat16)],
        compiler_params=pltpu.CompilerParams(dimension_semantics=("parallel",),
                                             vmem_limit_bytes=VMEM_LIMIT),
        name="in_proj",
    )(x2d, ng, w_lat, w_ret, qg, kvg, wuq, wukn, wuvt, *tabs)


def _retention_kernel(q_ref, k_ref, v_ref, dmask_ref, qdec_ref, kdec_ref, sdec_ref, gn_ref,
                      o_ref, local_ref, state_ref, score_ref):
    c = RET_CHUNK
    n = q_ref.shape[0] // c
    heads = range(o_ref.shape[1] // RET_DV)
    rows = lambda i: slice(i * c, (i + 1) * c)
    lanes = lambda hd: slice(hd * RET_DK, (hd + 1) * RET_DK)

    for hd in heads:
        kdec = kdec_ref[hd]
        for i in range(n):
            kf = k_ref[rows(i), lanes(hd)].astype(jnp.float32)
            kz = jnp.concatenate([_bf16(kf * kdec[:, :RET_DK]), _bf16(kf * kdec[:, RET_DK:])], axis=1)
            local_ref[hd, i] = lax.dot_general(kz, v_ref[rows(i), lanes(hd)], _TN,
                                               preferred_element_type=jnp.float32)

    for hd in heads:
        dmask = dmask_ref[hd]
        for i in range(n):
            s = lax.dot_general(q_ref[rows(i), lanes(hd)], k_ref[rows(i), lanes(hd)], _NT,
                                preferred_element_type=jnp.float32)
            score_ref[hd, i] = _bf16(s * dmask)

    for hd in heads:
        sdec = sdec_ref[hd]
        st = jnp.zeros((RET_DK, RET_DV), jnp.float32)
        for i in range(n):
            state_ref[hd, i, :RET_DK, :] = _bf16(st)
            st = st * sdec[:RET_DK] + local_ref[hd, i, :RET_DK, :]
        st = jnp.zeros((RET_DK, RET_DV), jnp.float32)
        for i in reversed(range(n)):
            state_ref[hd, i, RET_DK:, :] = _bf16(st)
            st = st * sdec[RET_DK:] + local_ref[hd, i, RET_DK:, :]

    for hd in heads:
        qdec = qdec_ref[hd]
        gain = gn_ref[:, lanes(hd)]
        for i in range(n):
            qf = q_ref[rows(i), lanes(hd)].astype(jnp.float32)
            qx = jnp.concatenate([_bf16(qf * qdec[:, :RET_DK]), _bf16(qf * qdec[:, RET_DK:])], axis=1)
            o = _dot(score_ref[hd, i], v_ref[rows(i), lanes(hd)]) + _dot(qx, state_ref[hd, i])
            mu = jnp.mean(o, axis=-1, keepdims=True)
            d = o - mu
            var = jnp.mean(d * d, axis=-1, keepdims=True)
            o_ref[rows(i), lanes(hd)] = _bf16(d * lax.rsqrt(var + NORM_EPS) * gain)


def _retention(rq, rk, rv, gn, dec):
    b, seq, _ = rq.shape
    c = RET_CHUNK
    n = seq // c
    hps = 2 if n <= RET_CHUNKS_PER_BLOCK // 2 else 1
    head = pl.BlockSpec((None, seq, RET_DK * hps), lambda i, h: (i, 0, h))
    per_head = lambda r, w: pl.BlockSpec((hps, r, w), lambda i, h: (h, 0, 0))
    return pl.pallas_call(
        _retention_kernel,
        grid=(b, RET_HEADS // hps),
        in_specs=[head, head, head, per_head(c, c), per_head(c, 2 * RET_DK), per_head(c, 2 * RET_DK),
                  per_head(2 * RET_DK, RET_DV), pl.BlockSpec((1, RET_DV * hps), lambda i, h: (0, h))],
        out_specs=head,
        out_shape=jax.ShapeDtypeStruct((b, seq, RET_WIDTH), jnp.bfloat16),
        scratch_shapes=[pltpu.VMEM((hps, n, 2 * RET_DK, RET_DV), jnp.float32),
                        pltpu.VMEM((hps, n, 2 * RET_DK, RET_DV), jnp.bfloat16),
                        pltpu.VMEM((hps, n, c, c), jnp.bfloat16)],
        compiler_params=pltpu.CompilerParams(dimension_semantics=("parallel", "arbitrary"),
                                             vmem_limit_bytes=VMEM_LIMIT),
        name="retention",
    )(rq, rk, rv, *dec, gn)


def _retention_tables():
    c = RET_CHUNK
    lg_f = jnp.log1p(-jnp.exp2(jnp.array(RET_LOG2_FWD, jnp.float32)))[:, None, None]
    lg_b = jnp.log1p(-jnp.exp2(jnp.array(RET_LOG2_BWD, jnp.float32)))[:, None, None]
    pos = jnp.arange(c, dtype=jnp.float32)
    diff = pos[:, None] - pos[None, :]
    dmask = jnp.where(diff >= 0, jnp.exp(lg_f * jnp.maximum(diff, 0.0)),
                      jnp.exp(lg_b * jnp.maximum(-diff, 0.0)))
    col = pos[None, :, None]
    wide = lambda f, b_: jnp.concatenate([jnp.broadcast_to(f, (RET_HEADS, c, RET_DK)),
                                          jnp.broadcast_to(b_, (RET_HEADS, c, RET_DK))], axis=2)
    qdec = wide(jnp.exp(lg_f * (col + 1.0)), jnp.exp(lg_b * (c - col)))
    kdec = wide(jnp.exp(lg_f * (c - 1.0 - col)), jnp.exp(lg_b * col))
    sdec = jnp.concatenate([jnp.broadcast_to(jnp.exp(lg_f * c), (RET_HEADS, RET_DK, RET_DV)),
                            jnp.broadcast_to(jnp.exp(lg_b * c), (RET_HEADS, RET_DK, RET_DV))], axis=1)
    return dmask, qdec, kdec, sdec


def _attention_kernel(q_ref, k_ref, vt_ref, o_ref):
    seq = q_ref.shape[0]
    heads = o_ref.shape[1] // MLA_V
    tq, tk = min(ATT_Q_TILE, seq), ATT_K_TILE
    n_k = seq // tk
    ws = ATT_Q_GROUP
    ones = jnp.ones((ATT_ONES_ROWS, tk), jnp.bfloat16)

    def q_tile(i, carry):
        qrows = pl.ds(pl.multiple_of(i * tq, tq), tq)
        streams = []
        for hd in range(heads):
            qt = _bf16(q_ref[qrows, hd * MLA_QK_PAD:(hd + 1) * MLA_QK_PAD].astype(jnp.float32).T)
            streams += [(hd, qt[:, c0:c0 + ws]) for c0 in range(0, tq, ws)]
        ns = len(streams)

        def scores(j, u):
            hd, qt_u = streams[u]
            return _dot(k_ref[j * tk:(j + 1) * tk, hd * MLA_QK_PAD:(hd + 1) * MLA_QK_PAD], qt_u)

        m = [jnp.full((1, ws), -jnp.inf, jnp.float32)] * ns
        acc = [jnp.zeros((MLA_V + ATT_ONES_ROWS, ws), jnp.float32)] * ns
        s = [scores(0, u) for u in range(ns)]
        for j in range(n_k):
            vt1 = [jnp.concatenate([vt_ref[hd * MLA_V:(hd + 1) * MLA_V, j * tk:(j + 1) * tk], ones], axis=0)
                   for hd in range(heads)]
            for u in range(ns):
                s_next = scores(j + 1, u) if j + 1 < n_k else None
                m_new = jnp.maximum(m[u], jnp.max(s[u], axis=0, keepdims=True))
                alpha = jnp.exp2(m[u] - m_new)
                p = _bf16(jnp.exp2(s[u] - m_new))
                acc[u] = alpha * acc[u] + _dot(vt1[streams[u][0]], p)
                m[u], s[u] = m_new, s_next
        per_head = ns // heads
        for hd in range(heads):
            out = jnp.concatenate([a[:MLA_V] / a[MLA_V:MLA_V + 1]
                                   for a in acc[hd * per_head:(hd + 1) * per_head]], axis=1)
            o_ref[qrows, hd * MLA_V:(hd + 1) * MLA_V] = _bf16(out.T)
        return carry

    lax.fori_loop(0, seq // tq, q_tile, 0)


def _attention(q, k, vt):
    b, seq, _ = q.shape
    hps = 2 if seq <= ATT_Q_TILE else 1
    head = lambda w: pl.BlockSpec((None, seq, w * hps), lambda i, h: (i, 0, h))
    return pl.pallas_call(
        _attention_kernel,
        grid=(b, MLA_HEADS // hps),
        in_specs=[head(MLA_QK_PAD), head(MLA_QK_PAD),
                  pl.BlockSpec((MLA_V * hps, seq), lambda i, h: (h, i))],
        out_specs=head(MLA_V),
        out_shape=jax.ShapeDtypeStruct((b, seq, MLA_WIDTH), jnp.bfloat16),
        compiler_params=pltpu.CompilerParams(dimension_semantics=("parallel", "arbitrary"),
                                             vmem_limit_bytes=VMEM_LIMIT),
        name="attention",
    )(q, k, vt)


def _out_proj_kernel(x_ref, ret_ref, att_ref, ng_ref, wg_ref, wbr_ref, wbm_ref, wo_ref, fg_ref,
                     y_ref, *, final_norm):
    x = x_ref[...]
    h = _bf16(_rms(x, ng_ref[...]))
    g = _dot(h, wg_ref[...])
    rg = g[:, :RET_WIDTH]
    mg = g[:, RET_WIDTH: RET_WIDTH + MLA_WIDTH]
    g_ret = g[:, RET_WIDTH + MLA_WIDTH: RET_WIDTH + MLA_WIDTH + D_MODEL]
    g_mla = g[:, RET_WIDTH + MLA_WIDTH + D_MODEL:]
    ret_out = _dot(_bf16(ret_ref[...].astype(jnp.float32) * (rg * jax.nn.sigmoid(rg))), wbr_ref[...])
    mla_out = _dot(_bf16(att_ref[...].astype(jnp.float32) * (mg * jax.nn.sigmoid(mg))), wbm_ref[...])
    merged = jax.nn.sigmoid(g_ret) * ret_out + jax.nn.sigmoid(g_mla) * mla_out
    y = x + _dot(_bf16(merged), wo_ref[...])
    if final_norm:
        y = _rms(y, fg_ref[...])
    y_ref[...] = y


def _out_proj(x2d, ret, att, ng, w_gate, w_br_ret, w_br_mla, w_out, fg, final_norm):
    tokens = x2d.shape[0]
    tm = ROW_TILE
    row = lambda width: pl.BlockSpec((tm, width), lambda i: (i, 0))
    return pl.pallas_call(
        functools.partial(_out_proj_kernel, final_norm=final_norm),
        grid=(tokens // tm,),
        in_specs=[row(D_MODEL), row(RET_WIDTH), row(MLA_WIDTH), _resident((1, D_MODEL)),
                  _resident(w_gate.shape), _resident(w_br_ret.shape), _resident(w_br_mla.shape),
                  _resident(w_out.shape), _resident((1, D_MODEL))],
        out_specs=row(D_MODEL),
        out_shape=jax.ShapeDtypeStruct((tokens, D_MODEL), jnp.float32),
        compiler_params=pltpu.CompilerParams(dimension_semantics=("parallel",),
                                             vmem_limit_bytes=VMEM_LIMIT),
        name="out_proj",
    )(x2d, ret, att, ng, w_gate, w_br_ret, w_br_mla, w_out, fg)


def _rope_tables(seq):
    posn = jnp.arange(seq, dtype=jnp.float32)[:, None]

    def angles(d):
        inv = ROPE_BASE ** (-jnp.arange(0, d, 2, dtype=jnp.float32) / d)
        return posn * inv[None, :]

    ar = angles(RET_DK)
    cos_r = jnp.concatenate([jnp.cos(ar), jnp.cos(ar)], axis=1)
    sin_r = jnp.concatenate([-jnp.sin(ar), jnp.sin(ar)], axis=1)
    ap = angles(MLA_ROPE)
    z32 = jnp.zeros_like(ap)
    z64 = jnp.zeros((seq, LANES - MLA_ROPE), jnp.float32)
    cos_p = jnp.concatenate([jnp.cos(ap), jnp.cos(ap), z64], axis=1)
    sin_lo = jnp.concatenate([-jnp.sin(ap), z32, z64], axis=1)
    sin_hi = jnp.concatenate([z32, jnp.sin(ap), z64], axis=1)
    return cos_r, sin_r, cos_p, sin_lo, sin_hi


def _prep_layer(w_in, w_uq, w_ukv):
    zpad = jnp.zeros((D_MODEL, LANES - MLA_ROPE), w_in.dtype)
    w_lat = jnp.concatenate([w_in[:, O_CQ:O_MG], zpad], axis=1)
    w_ret = w_in[:, O_RQ:O_RG]
    w_gate = jnp.concatenate([w_in[:, O_RG:O_CQ], w_in[:, O_MG:]], axis=1)
    uq = w_uq.reshape(Q_LORA, MLA_HEADS, MLA_QK)
    uq = jnp.concatenate([uq, jnp.zeros((Q_LORA, MLA_HEADS, MLA_QK_PAD - MLA_QK), w_uq.dtype)], axis=2)
    ukv = w_ukv.reshape(KV_LORA, MLA_HEADS, MLA_NOPE + MLA_V)
    ukn = ukv[:, :, :MLA_NOPE].reshape(KV_LORA, -1)
    uvt = ukv[:, :, MLA_NOPE:].reshape(KV_LORA, -1).T
    return (_bf16(w_lat), _bf16(w_ret), _bf16(w_gate), _bf16(uq.reshape(Q_LORA, -1)), _bf16(ukn),
            _bf16(uvt))


def _trunk(x, layers, final_norm_g, ret_tabs):
    b, seq, _ = x.shape
    rope_tabs = _rope_tables(seq)
    x2d = x.reshape(b * seq, D_MODEL)
    for i, lw in enumerate(layers):
        rq, rk, rv, q, k, vt = _in_proj(x2d, seq, lw["ng"], lw["w_lat"], lw["w_ret"], lw["qg"],
                                        lw["kvg"], lw["wuq"], lw["wukn"], lw["wuvt"], rope_tabs)
        sh = lambda t: t.reshape(b, seq, t.shape[-1])
        ret = _retention(sh(rq), sh(rk), sh(rv), lw["gn"], ret_tabs)
        att = _attention(sh(q), sh(k), vt)
        x2d = _out_proj(x2d, ret.reshape(b * seq, -1), att.reshape(b * seq, -1), lw["ng"], lw["w_gate"],
                        lw["wbr"], lw["wbm"], lw["wo"], final_norm_g, final_norm=(i == DEPTH - 1))
    return x2d.reshape(b, seq, D_MODEL)


def kernel(x_prompt, x_sample, norm_g, w_in, ret_gn_g, q_norm_g, kv_norm_g, w_uq, w_ukv, w_br_ret, w_br_mla,
           w_out, final_norm_g):
    layers = []
    for i in range(DEPTH):
        w_lat, w_ret, w_gate, wuq, wukn, wuvt = _prep_layer(w_in[i], w_uq[i], w_ukv[i])
        qg = q_norm_g[i][None, :] * (MLA_QK ** -0.5 * math.log2(math.e))
        layers.append(dict(ng=norm_g[i][None, :], qg=qg, kvg=kv_norm_g[i][None, :],
                           gn=ret_gn_g[i][None, :], w_lat=w_lat, w_ret=w_ret, w_gate=w_gate, wuq=wuq, wukn=wukn,
                           wuvt=wuvt,
                           wbr=_bf16(w_br_ret[i]), wbm=_bf16(w_br_mla[i]), wo=_bf16(w_out[i])))
    fg = final_norm_g[None, :]
    ret_tabs = _retention_tables()
    return (_trunk(x_prompt, layers, fg, ret_tabs), _trunk(x_sample, layers, fg, ret_tabs))
```

```python
import functools
import math

import jax
import jax.numpy as jnp
import numpy as np
from jax import lax
from jax.experimental import pallas as pl
from jax.experimental.pallas import tpu as pltpu

D_MODEL = 1024
DEPTH = 2
RET_HEADS = 4
RET_DK = 128
RET_DV = 128
RET_WIDTH = RET_HEADS * RET_DV
RET_LOG2_FWD = (-5.0, -6.0, -7.0, -8.0)
RET_LOG2_BWD = (-5.5, -6.5, -7.5, -8.5)
MLA_HEADS = 4
MLA_NOPE = 128
MLA_ROPE = 64
MLA_QK = MLA_NOPE + MLA_ROPE
MLA_V = 128
MLA_WIDTH = MLA_HEADS * MLA_V
Q_LORA = 256
KV_LORA = 128
ROPE_BASE = 10000.0
NORM_EPS = 1e-6

LANES = 128
MLA_QK_PAD = 2 * LANES
_OFF = np.cumsum((0, RET_HEADS * RET_DK, RET_HEADS * RET_DK, RET_WIDTH, RET_WIDTH, Q_LORA, KV_LORA,
                  MLA_ROPE, MLA_WIDTH, D_MODEL, D_MODEL))
(O_RQ, O_RK, O_RV, O_RG, O_CQ, O_CKV, O_KPE, O_MG, O_GRET, O_GMLA, _) = [int(v) for v in _OFF]

ROW_TILE = 1024
RET_CHUNK = 256
RET_CHUNKS_PER_BLOCK = 16
ATT_Q_TILE = 2048
ATT_K_TILE = 512
ATT_Q_GROUP = 256
ATT_ONES_ROWS = 16
VMEM_LIMIT = 56 * 1024 * 1024

_NT = (((1,), (1,)), ((), ()))
_TN = (((0,), (0,)), ((), ()))


def _bf16(x):
    return x.astype(jnp.bfloat16)


def _dot(a, b):
    return jnp.dot(a, b, preferred_element_type=jnp.float32)


def _rms(x, g):
    return x * lax.rsqrt(jnp.mean(x * x, axis=-1, keepdims=True) + NORM_EPS) * g


def _resident(shape):
    return pl.BlockSpec(shape, lambda *_: (0,) * len(shape))


def _in_proj_kernel(x_ref, ng_ref, wlat_ref, wret_ref, qg_ref, kvg_ref, wuq_ref, wukn_ref, wuvt_ref,
                    cosr_ref, sinr_ref, cosp_ref, sinlo_ref, sinhi_ref,
                    rq_ref, rk_ref, rv_ref, q_ref, k_ref, vt_ref):
    h = _bf16(_rms(x_ref[...], ng_ref[...]))
    zl = _dot(h, wlat_ref[...])
    zr = _dot(h, wret_ref[...])

    cq = _bf16(_rms(zl[:, :Q_LORA], qg_ref[...]))
    ckv = _bf16(_rms(zl[:, Q_LORA: Q_LORA + KV_LORA], kvg_ref[...]))
    q = _dot(cq, wuq_ref[...])
    kn = _dot(ckv, wukn_ref[...])
    vt_ref[...] = _bf16(lax.dot_general(wuvt_ref[...], ckv, _NT, preferred_element_type=jnp.float32))

    cosr, sinr = cosr_ref[...], sinr_ref[...]
    k_scale = RET_DK ** -0.5
    for hd in range(RET_HEADS):
        sl = slice(hd * RET_DK, (hd + 1) * RET_DK)
        xq = zr[:, sl]
        rq_ref[:, sl] = _bf16(xq * cosr + pltpu.roll(xq, RET_DK // 2, 1) * sinr)
        xk = zr[:, RET_WIDTH + hd * RET_DK: RET_WIDTH + (hd + 1) * RET_DK]
        rk_ref[:, sl] = _bf16((xk * cosr + pltpu.roll(xk, RET_DK // 2, 1) * sinr) * k_scale)
    rv_ref[...] = _bf16(zr[:, 2 * RET_WIDTH:])

    cosp, sinlo, sinhi = cosp_ref[...], sinlo_ref[...], sinhi_ref[...]

    def rope64(pe):
        return (pe * cosp + pltpu.roll(pe, LANES - MLA_ROPE // 2, 1) * sinlo
                + pltpu.roll(pe, MLA_ROPE // 2, 1) * sinhi)

    kpe = _bf16(rope64(zl[:, Q_LORA + KV_LORA:]))
    for hd in range(MLA_HEADS):
        b0 = hd * MLA_QK_PAD
        k_ref[:, b0: b0 + LANES] = _bf16(kn[:, hd * MLA_NOPE: (hd + 1) * MLA_NOPE])
        k_ref[:, b0 + LANES: b0 + 2 * LANES] = kpe
        q_ref[:, b0: b0 + LANES] = _bf16(q[:, b0: b0 + LANES])
        q_ref[:, b0 + LANES: b0 + 2 * LANES] = _bf16(rope64(q[:, b0 + LANES: b0 + 2 * LANES]))


def _in_proj(x2d, seq, ng, w_lat, w_ret, qg, kvg, wuq, wukn, wuvt, tabs):
    tokens = x2d.shape[0]
    tm = ROW_TILE
    pos_tiles = seq // tm
    row = lambda width: pl.BlockSpec((tm, width), lambda i: (i, 0))
    pos = pl.BlockSpec((tm, LANES), lambda i: (i % pos_tiles, 0))
    bf = lambda width: jax.ShapeDtypeStruct((tokens, width), jnp.bfloat16)
    return pl.pallas_call(
        _in_proj_kernel,
        grid=(tokens // tm,),
        in_specs=[row(D_MODEL), _resident((1, D_MODEL)), _resident(w_lat.shape), _resident(w_ret.shape),
                  _resident((1, Q_LORA)), _resident((1, KV_LORA)),
                  _resident(wuq.shape), _resident(wukn.shape), _resident(wuvt.shape), pos, pos, pos, pos, pos],
        out_specs=[row(RET_WIDTH), row(RET_WIDTH), row(RET_WIDTH), row(MLA_HEADS * MLA_QK_PAD),
                   row(MLA_HEADS * MLA_QK_PAD), pl.BlockSpec((MLA_WIDTH, tm), lambda i: (0, i))],
        out_shape=[bf(RET_WIDTH), bf(RET_WIDTH), bf(RET_WIDTH), bf(MLA_HEADS * MLA_QK_PAD),
                   bf(MLA_HEADS * MLA_QK_PAD), jax.ShapeDtypeStruct((MLA_WIDTH, tokens), jnp.bfloat16)],
        compiler_params=pltpu.CompilerParams(dimension_semantics=("parallel",),
                                             vmem_limit_bytes=VMEM_LIMIT),
        name="in_proj",
    )(x2d, ng, w_lat, w_ret, qg, kvg, wuq, wukn, wuvt, *tabs)


def _retention_kernel(q_ref, k_ref, v_ref, dmask_ref, qdec_ref, kdec_ref, sdec_ref, gn_ref,
                      o_ref, local_ref, state_ref, score_ref):
    c = RET_CHUNK
    n = q_ref.shape[0] // c
    heads = range(o_ref.shape[1] // RET_DV)
    rows = lambda i: slice(i * c, (i + 1) * c)
    lanes = lambda hd: slice(hd * RET_DK, (hd + 1) * RET_DK)

    for hd in heads:
        kdec = kdec_ref[hd]
        for i in range(n):
            kf = k_ref[rows(i), lanes(hd)].astype(jnp.float32)
            kz = jnp.concatenate([_bf16(kf * kdec[:, :RET_DK]), _bf16(kf * kdec[:, RET_DK:])], axis=1)
            local_ref[hd, i] = lax.dot_general(kz, v_ref[rows(i), lanes(hd)], _TN,
                                               preferred_element_type=jnp.float32)

    for hd in heads:
        dmask = dmask_ref[hd]
        for i in range(n):
            s = lax.dot_general(q_ref[rows(i), lanes(hd)], k_ref[rows(i), lanes(hd)], _NT,
                                preferred_element_type=jnp.float32)
            score_ref[hd, i] = _bf16(s * dmask)

    for hd in heads:
        sdec = sdec_ref[hd]
        st = jnp.zeros((RET_DK, RET_DV), jnp.float32)
        for i in range(n):
            state_ref[hd, i, :RET_DK, :] = _bf16(st)
            st = st * sdec[:RET_DK] + local_ref[hd, i, :RET_DK, :]
        st = jnp.zeros((RET_DK, RET_DV), jnp.float32)
        for i in reversed(range(n)):
            state_ref[hd, i, RET_DK:, :] = _bf16(st)
            st = st * sdec[RET_DK:] + local_ref[hd, i, RET_DK:, :]

    for hd in heads:
        qdec = qdec_ref[hd]
        gain = gn_ref[:, lanes(hd)]
        for i in range(n):
            qf = q_ref[rows(i), lanes(hd)].astype(jnp.float32)
            qx = jnp.concatenate([_bf16(qf * qdec[:, :RET_DK]), _bf16(qf * qdec[:, RET_DK:])], axis=1)
            o = _dot(score_ref[hd, i], v_ref[rows(i), lanes(hd)]) + _dot(qx, state_ref[hd, i])
            mu = jnp.mean(o, axis=-1, keepdims=True)
            d = o - mu
            var = jnp.mean(d * d, axis=-1, keepdims=True)
            o_ref[rows(i), lanes(hd)] = _bf16(d * lax.rsqrt(var + NORM_EPS) * gain)


def _retention(rq, rk, rv, gn, dec):
    b, seq, _ = rq.shape
    c = RET_CHUNK
    n = seq // c
    hps = 2 if n <= RET_CHUNKS_PER_BLOCK // 2 else 1
    head = pl.BlockSpec((None, seq, RET_DK * hps), lambda i, h: (i, 0, h))
    per_head = lambda r, w: pl.BlockSpec((hps, r, w), lambda i, h: (h, 0, 0))
    return pl.pallas_call(
        _retention_kernel,
        grid=(b, RET_HEADS // hps),
        in_specs=[head, head, head, per_head(c, c), per_head(c, 2 * RET_DK), per_head(c, 2 * RET_DK),
                  per_head(2 * RET_DK, RET_DV), pl.BlockSpec((1, RET_DV * hps), lambda i, h: (0, h))],
        out_specs=head,
        out_shape=jax.ShapeDtypeStruct((b, seq, RET_WIDTH), jnp.bfloat16),
        scratch_shapes=[pltpu.VMEM((hps, n, 2 * RET_DK, RET_DV), jnp.float32),
                        pltpu.VMEM((hps, n, 2 * RET_DK, RET_DV), jnp.bfloat16),
                        pltpu.VMEM((hps, n, c, c), jnp.bfloat16)],
        compiler_params=pltpu.CompilerParams(dimension_semantics=("parallel", "arbitrary"),
                                             vmem_limit_bytes=VMEM_LIMIT),
        name="retention",
    )(rq, rk, rv, *dec, gn)


def _retention_tables():
    c = RET_CHUNK
    lg_f = jnp.log1p(-jnp.exp2(jnp.array(RET_LOG2_FWD, jnp.float32)))[:, None, None]
    lg_b = jnp.log1p(-jnp.exp2(jnp.array(RET_LOG2_BWD, jnp.float32)))[:, None, None]
    pos = jnp.arange(c, dtype=jnp.float32)
    diff = pos[:, None] - pos[None, :]
    dmask = jnp.where(diff >= 0, jnp.exp(lg_f * jnp.maximum(diff, 0.0)),
                      jnp.exp(lg_b * jnp.maximum(-diff, 0.0)))
    col = pos[None, :, None]
    wide = lambda f, b_: jnp.concatenate([jnp.broadcast_to(f, (RET_HEADS, c, RET_DK)),
                                          jnp.broadcast_to(b_, (RET_HEADS, c, RET_DK))], axis=2)
    qdec = wide(jnp.exp(lg_f * (col + 1.0)), jnp.exp(lg_b * (c - col)))
    kdec = wide(jnp.exp(lg_f * (c - 1.0 - col)), jnp.exp(lg_b * col))
    sdec = jnp.concatenate([jnp.broadcast_to(jnp.exp(lg_f * c), (RET_HEADS, RET_DK, RET_DV)),
                            jnp.broadcast_to(jnp.exp(lg_b * c), (RET_HEADS, RET_DK, RET_DV))], axis=1)
    return dmask, qdec, kdec, sdec


def _attention_kernel(q_ref, k_ref, vt_ref, o_ref):
    seq = q_ref.shape[0]
    heads = o_ref.shape[1] // MLA_V
    tq, tk = min(ATT_Q_TILE, seq), ATT_K_TILE
    n_k = seq // tk
    ws = ATT_Q_GROUP
    ones = jnp.ones((ATT_ONES_ROWS, tk), jnp.bfloat16)

    def q_tile(i, carry):
        qrows = pl.ds(pl.multiple_of(i * tq, tq), tq)
        streams = []
        for hd in range(heads):
            qt = _bf16(q_ref[qrows, hd * MLA_QK_PAD:(hd + 1) * MLA_QK_PAD].astype(jnp.float32).T)
            streams += [(hd, qt[:, c0:c0 + ws]) for c0 in range(0, tq, ws)]
        ns = len(streams)

        def scores(j, u):
            hd, qt_u = streams[u]
            return _dot(k_ref[j * tk:(j + 1) * tk, hd * MLA_QK_PAD:(hd + 1) * MLA_QK_PAD], qt_u)

        m = [jnp.full((1, ws), -jnp.inf, jnp.float32)] * ns
        acc = [jnp.zeros((MLA_V + ATT_ONES_ROWS, ws), jnp.float32)] * ns
        s = [scores(0, u) for u in range(ns)]
        for j in range(n_k):
            vt1 = [jnp.concatenate([vt_ref[hd * MLA_V:(hd + 1) * MLA_V, j * tk:(j + 1) * tk], ones], axis=0)
                   for hd in range(heads)]
            for u in range(ns):
                s_next = scores(j + 1, u) if j + 1 < n_k else None
                m_new = jnp.maximum(m[u], jnp.max(s[u], axis=0, keepdims=True))
                alpha = jnp.exp2(m[u] - m_new)
                p = _bf16(jnp.exp2(s[u] - m_new))
                acc[u] = alpha * acc[u] + _dot(vt1[streams[u][0]], p)
                m[u], s[u] = m_new, s_next
        per_head = ns // heads
        for hd in range(heads):
            out = jnp.concatenate([a[:MLA_V] / a[MLA_V:MLA_V + 1]
                                   for a in acc[hd * per_head:(hd + 1) * per_head]], axis=1)
            o_ref[qrows, hd * MLA_V:(hd + 1) * MLA_V] = _bf16(out.T)
        return carry

    lax.fori_loop(0, seq // tq, q_tile, 0)


def _attention(q, k, vt):
    b, seq, _ = q.shape
    hps = 2 if seq <= ATT_Q_TILE else 1
    head = lambda w: pl.BlockSpec((None, seq, w * hps), lambda i, h: (i, 0, h))
    return pl.pallas_call(
        _attention_kernel,
        grid=(b, MLA_HEADS // hps),
        in_specs=[head(MLA_QK_PAD), head(MLA_QK_PAD),
                  pl.BlockSpec((MLA_V * hps, seq), lambda i, h: (h, i))],
        out_specs=head(MLA_V),
        out_shape=jax.ShapeDtypeStruct((b, seq, MLA_WIDTH), jnp.bfloat16),
        compiler_params=pltpu.CompilerParams(dimension_semantics=("parallel", "arbitrary"),
                                             vmem_limit_bytes=VMEM_LIMIT),
        name="attention",
    )(q, k, vt)


def _out_proj_kernel(x_ref, ret_ref, att_ref, ng_ref, wg_ref, wbr_ref, wbm_ref, wo_ref, fg_ref,
                     y_ref, *, final_norm):
    x = x_ref[...]
    h = _bf16(_rms(x, ng_ref[...]))
    g = _dot(h, wg_ref[...])
    rg = g[:, :RET_WIDTH]
    mg = g[:, RET_WIDTH: RET_WIDTH + MLA_WIDTH]
    g_ret = g[:, RET_WIDTH + MLA_WIDTH: RET_WIDTH + MLA_WIDTH + D_MODEL]
    g_mla = g[:, RET_WIDTH + MLA_WIDTH + D_MODEL:]
    ret_out = _dot(_bf16(ret_ref[...].astype(jnp.float32) * (rg * jax.nn.sigmoid(rg))), wbr_ref[...])
    mla_out = _dot(_bf16(att_ref[...].astype(jnp.float32) * (mg * jax.nn.sigmoid(mg))), wbm_ref[...])
    merged = jax.nn.sigmoid(g_ret) * ret_out + jax.nn.sigmoid(g_mla) * mla_out
    y = x + _dot(_bf16(merged), wo_ref[...])
    if final_norm:
        y = _rms(y, fg_ref[...])
    y_ref[...] = y


def _out_proj(x2d, ret, att, ng, w_gate, w_br_ret, w_br_mla, w_out, fg, final_norm):
    tokens = x2d.shape[0]
    tm = ROW_TILE
    row = lambda width: pl.BlockSpec((tm, width), lambda i: (i, 0))
    return pl.pallas_call(
        functools.partial(_out_proj_kernel, final_norm=final_norm),
        grid=(tokens // tm,),
        in_specs=[row(D_MODEL), row(RET_WIDTH), row(MLA_WIDTH), _resident((1, D_MODEL)),
                  _resident(w_gate.shape), _resident(w_br_ret.shape), _resident(w_br_mla.shape),
                  _resident(w_out.shape), _resident((1, D_MODEL))],
        out_specs=row(D_MODEL),
        out_shape=jax.ShapeDtypeStruct((tokens, D_MODEL), jnp.float32),
        compiler_params=pltpu.CompilerParams(dimension_semantics=("parallel",),
                                             vmem_limit_bytes=VMEM_LIMIT),
        name="out_proj",
    )(x2d, ret, att, ng, w_gate, w_br_ret, w_br_mla, w_out, fg)


def _rope_tables(seq):
    posn = jnp.arange(seq, dtype=jnp.float32)[:, None]

    def angles(d):
        inv = ROPE_BASE ** (-jnp.arange(0, d, 2, dtype=jnp.float32) / d)
        return posn * inv[None, :]

    ar = angles(RET_DK)
    cos_r = jnp.concatenate([jnp.cos(ar), jnp.cos(ar)], axis=1)
    sin_r = jnp.concatenate([-jnp.sin(ar), jnp.sin(ar)], axis=1)
    ap = angles(MLA_ROPE)
    z32 = jnp.zeros_like(ap)
    z64 = jnp.zeros((seq, LANES - MLA_ROPE), jnp.float32)
    cos_p = jnp.concatenate([jnp.cos(ap), jnp.cos(ap), z64], axis=1)
    sin_lo = jnp.concatenate([-jnp.sin(ap), z32, z64], axis=1)
    sin_hi = jnp.concatenate([z32, jnp.sin(ap), z64], axis=1)
    return cos_r, sin_r, cos_p, sin_lo, sin_hi


def _prep_layer(w_in, w_uq, w_ukv):
    zpad = jnp.zeros((D_MODEL, LANES - MLA_ROPE), w_in.dtype)
    w_lat = jnp.concatenate([w_in[:, O_CQ:O_MG], zpad], axis=1)
    w_ret = w_in[:, O_RQ:O_RG]
    w_gate = jnp.concatenate([w_in[:, O_RG:O_CQ], w_in[:, O_MG:]], axis=1)
    uq = w_uq.reshape(Q_LORA, MLA_HEADS, MLA_QK)
    uq = jnp.concatenate([uq, jnp.zeros((Q_LORA, MLA_HEADS, MLA_QK_PAD - MLA_QK), w_uq.dtype)], axis=2)
    ukv = w_ukv.reshape(KV_LORA, MLA_HEADS, MLA_NOPE + MLA_V)
    ukn = ukv[:, :, :MLA_NOPE].reshape(KV_LORA, -1)
    uvt = ukv[:, :, MLA_NOPE:].reshape(KV_LORA, -1).T
    return (_bf16(w_lat), _bf16(w_ret), _bf16(w_gate), _bf16(uq.reshape(Q_LORA, -1)), _bf16(ukn),
            _bf16(uvt))


def _trunk(x, layers, final_norm_g, ret_tabs):
    b, seq, _ = x.shape
    rope_tabs = _rope_tables(seq)
    x2d = x.reshape(b * seq, D_MODEL)
    for i, lw in enumerate(layers):
        rq, rk, rv, q, k, vt = _in_proj(x2d, seq, lw["ng"], lw["w_lat"], lw["w_ret"], lw["qg"],
                                        lw["kvg"], lw["wuq"], lw["wukn"], lw["wuvt"], rope_tabs)
        sh = lambda t: t.reshape(b, seq, t.shape[-1])
        ret = _retention(sh(rq), sh(rk), sh(rv), lw["gn"], ret_tabs)
        att = _attention(sh(q), sh(k), vt)
        x2d = _out_proj(x2d, ret.reshape(b * seq, -1), att.reshape(b * seq, -1), lw["ng"], lw["w_gate"],
                        lw["wbr"], lw["wbm"], lw["wo"], final_norm_g, final_norm=(i == DEPTH - 1))
    return x2d.reshape(b, seq, D_MODEL)


def kernel(x_prompt, x_sample, norm_g, w_in, ret_gn_g, q_norm_g, kv_norm_g, w_uq, w_ukv, w_br_ret, w_br_mla,
           w_out, final_norm_g):
    layers = []
    for i in range(DEPTH):
        w_lat, w_ret, w_gate, wuq, wukn, wuvt = _prep_layer(w_in[i], w_uq[i], w_ukv[i])
        qg = q_norm_g[i][None, :] * (MLA_QK ** -0.5 * math.log2(math.e))
        layers.append(dict(ng=norm_g[i][None, :], qg=qg, kvg=kv_norm_g[i][None, :],
                           gn=ret_gn_g[i][None, :], w_lat=w_lat, w_ret=w_ret, w_gate=w_gate, wuq=wuq, wukn=wukn,
                           wuvt=wuvt,
                           wbr=_bf16(w_br_ret[i]), wbm=_bf16(w_br_mla[i]), wo=_bf16(w_out[i])))
    fg = final_norm_g[None, :]
    ret_tabs = _retention_tables()
    return (_trunk(x_prompt, layers, fg, ret_tabs), _trunk(x_sample, layers, fg, ret_tabs))
```
